```python
import math
import jax, jax.numpy as jnp
from jax import lax
import numpy as np

D_MODEL = 1024
BATCH = 8
SEQ = 4096
DEPTH = 1

MIX_WIDTH = D_MODEL
HEAD_DIM = 64
GMLP_GROUPS = 8
GMLP_WIDTH = GMLP_GROUPS * HEAD_DIM
CHUNK = 128
N_Q_HEADS = 8
N_KV_HEADS = 2
GQA_GROUP = N_Q_HEADS // N_KV_HEADS
ATTN_WIDTH = N_Q_HEADS * HEAD_DIM
KV_WIDTH = N_KV_HEADS * HEAD_DIM
WINDOW = 128
Q_BLOCK = 128
ROPE_THETA = 500000.0
ROT_DIM = HEAD_DIM // 4
D_FF = 4 * D_MODEL
IN_PROJ_WIDTH = 2 * GMLP_WIDTH + ATTN_WIDTH + 2 * KV_WIDTH
N_MOD = 6
EPS = 1e-5

kernel_name = "hybrid_gmlp_swa_sink_block"


def rms_norm(x, g):
    xf = x.astype(jnp.float32)
    y = xf * lax.rsqrt(jnp.mean(xf * xf, axis=-1, keepdims=True) + EPS)
    return (y * g.astype(jnp.float32)).astype(x.dtype)


def modulate(h, shift, scale):
    return h * (1 + scale[:, None, :]) + shift[:, None, :]


def partial_rope(t, positions):
    half = ROT_DIM // 2
    inv_freq = ROPE_THETA ** (-jnp.arange(0, ROT_DIM, 2, dtype=jnp.float32) / ROT_DIM)
    ang = positions.astype(jnp.float32)[..., None] * inv_freq
    cos = jnp.cos(ang)[:, :, None, :].astype(t.dtype)
    sin = jnp.sin(ang)[:, :, None, :].astype(t.dtype)
    t1 = t[..., :half]
    t2 = t[..., half:ROT_DIM]
    return jnp.concatenate([t1 * cos - t2 * sin, t2 * cos + t1 * sin, t[..., ROT_DIM:]], axis=-1)


def chunked_sgu(z, w_s, b_s):
    B, S, _ = z.shape
    n_chunks = S // CHUNK
    u, v = jnp.split(z, 2, axis=-1)
    v = v.reshape(B, n_chunks, CHUNK, GMLP_GROUPS, HEAD_DIM)
    causal = jnp.tril(jnp.ones((CHUNK, CHUNK), dtype=w_s.dtype))
    w = w_s * causal[None]
    sv = jnp.einsum('hts,bcshd->bcthd', w, v) + jnp.transpose(b_s)[None, None, :, :, None]
    return u * sv.reshape(B, S, GMLP_WIDTH)


def sliding_window_sink_attention(q, k, v, sinks, positions):
    B, S, _, _ = q.shape
    nb = S // Q_BLOCK
    q = partial_rope(q, positions)
    k = partial_rope(k, positions)
    qb = q.reshape(B, nb, Q_BLOCK, N_KV_HEADS, GQA_GROUP, HEAD_DIM)
    pad = jnp.zeros((B, Q_BLOCK, N_KV_HEADS, HEAD_DIM), k.dtype)
    kp = jnp.concatenate([pad, k], axis=1)
    vp = jnp.concatenate([pad, v], axis=1)
    kb = jnp.concatenate([kp[:, :S].reshape(B, nb, Q_BLOCK, N_KV_HEADS, HEAD_DIM),
                          k.reshape(B, nb, Q_BLOCK, N_KV_HEADS, HEAD_DIM)], axis=2)
    vb = jnp.concatenate([vp[:, :S].reshape(B, nb, Q_BLOCK, N_KV_HEADS, HEAD_DIM),
                          v.reshape(B, nb, Q_BLOCK, N_KV_HEADS, HEAD_DIM)], axis=2)
    scores = jnp.einsum('bnqhgd,bnkhd->bnhgqk', qb, kb).astype(jnp.float32) / math.sqrt(HEAD_DIM)
    qi = jnp.arange(Q_BLOCK)[:, None]
    kj = jnp.arange(2 * Q_BLOCK)[None, :]
    band = (kj > qi) & (kj <= qi + WINDOW)
    valid_first = jnp.arange(nb)[:, None, None] > 0
    mask = band[None] & (valid_first | (kj >= Q_BLOCK)[None])
    mask = mask[None, :, None, None]
    scores = jnp.where(mask, scores, -jnp.inf)
    sink = sinks.astype(jnp.float32).reshape(N_KV_HEADS, GQA_GROUP)[None, None, :, :, None, None]
    m = jnp.maximum(jnp.max(scores, axis=-1, keepdims=True), sink)
    p = jnp.exp(scores - m)
    denom = jnp.sum(p, axis=-1, keepdims=True) + jnp.exp(sink - m)
    probs = (p / denom).astype(v.dtype)
    out = jnp.einsum('bnhgqk,bnkhd->bnqhgd', probs, vb)
    return out.reshape(B, S, ATTN_WIDTH)


def setup_inputs(seed: int = 0) -> dict:
    key = jax.random.key(seed)
    ks = jax.random.split(key, 16)
    f32 = jnp.float32
    x = jax.random.normal(ks[0], (BATCH, SEQ, D_MODEL), f32)
    c = jax.random.normal(ks[1], (BATCH, D_MODEL), f32)
    offsets = jax.random.randint(ks[2], (BATCH, 1), 0, 2048, dtype=jnp.int32)
    positions = offsets + jnp.arange(SEQ, dtype=jnp.int32)[None, :]
    w_ada = jax.random.normal(ks[3], (DEPTH, D_MODEL, N_MOD * D_MODEL), f32) * (0.5 * D_MODEL ** -0.5)
    b_ada = jax.random.normal(ks[4], (DEPTH, N_MOD * D_MODEL), f32) * 0.02
    g_mix = 1.0 + 0.02 * jax.random.normal(ks[5], (DEPTH, D_MODEL), f32)
    w_in = jax.random.normal(ks[6], (DEPTH, D_MODEL, IN_PROJ_WIDTH), f32) * D_MODEL ** -0.5
    w_spatial = jax.random.normal(ks[7], (DEPTH, GMLP_GROUPS, CHUNK, CHUNK), f32) * CHUNK ** -0.5
    b_spatial = 1.0 + 0.01 * jax.random.normal(ks[8], (DEPTH, GMLP_GROUPS, CHUNK), f32)
    sinks = jax.random.normal(ks[9], (DEPTH, N_Q_HEADS), f32) * 0.5
    w_out = jax.random.normal(ks[10], (DEPTH, MIX_WIDTH, D_MODEL), f32) * MIX_WIDTH ** -0.5
    g_ffn = 1.0 + 0.02 * jax.random.normal(ks[11], (DEPTH, D_MODEL), f32)
    w_ff1 = jax.random.normal(ks[12], (DEPTH, D_MODEL, D_FF), f32) * D_MODEL ** -0.5
    w_ff2 = jax.random.normal(ks[13], (DEPTH, D_FF, D_MODEL), f32) * D_FF ** -0.5
    g_final = 1.0 + 0.02 * jax.random.normal(ks[14], (D_MODEL,), f32)
    return {"x": x, "c": c, "positions": positions, "w_ada": w_ada, "b_ada": b_ada,
            "g_mix": g_mix, "w_in": w_in, "w_spatial": w_spatial, "b_spatial": b_spatial,
            "sinks": sinks, "w_out": w_out, "g_ffn": g_ffn, "w_ff1": w_ff1, "w_ff2": w_ff2,
            "g_final": g_final}


def reference(x, c, positions, w_ada, b_ada, g_mix, w_in, w_spatial, b_spatial, sinks,
              w_out, g_ffn, w_ff1, w_ff2, g_final):
    B, S, _ = x.shape
    c_act = jax.nn.silu(c)
    for l in range(DEPTH):
        mod = c_act @ w_ada[l] + b_ada[l]
        shift1, scale1, gate1, shift2, scale2, gate2 = jnp.split(mod, N_MOD, axis=-1)

        h = modulate(rms_norm(x, g_mix[l]), shift1, scale1)
        proj = h @ w_in[l]
        z_a = proj[..., :2 * GMLP_WIDTH]
        o = 2 * GMLP_WIDTH
        q = proj[..., o:o + ATTN_WIDTH].reshape(B, S, N_Q_HEADS, HEAD_DIM)
        o += ATTN_WIDTH
        k = proj[..., o:o + KV_WIDTH].reshape(B, S, N_KV_HEADS, HEAD_DIM)
        o += KV_WIDTH
        v = proj[..., o:o + KV_WIDTH].reshape(B, S, N_KV_HEADS, HEAD_DIM)

        out_a = chunked_sgu(jax.nn.gelu(z_a), w_spatial[l], b_spatial[l])
        out_b = sliding_window_sink_attention(q, k, v, sinks[l], positions)
        mix = jnp.concatenate([out_a, out_b], axis=-1) @ w_out[l]
        x = x + gate1[:, None, :] * mix

        h2 = modulate(rms_norm(x, g_ffn[l]), shift2, scale2)
        ff = jnp.square(jax.nn.relu(h2 @ w_ff1[l])) @ w_ff2[l]
        x = x + gate2[:, None, :] * ff
    return rms_norm(x, g_final)
```

```python
import functools
import math

import numpy as np
import jax
import jax.numpy as jnp
from jax import lax
from jax.experimental import pallas as pl
from jax.experimental.pallas import tpu as pltpu

D_MODEL = 1024
HEAD_DIM = 64
GMLP_GROUPS = 8
GMLP_WIDTH = GMLP_GROUPS * HEAD_DIM
CHUNK = 128
N_Q_HEADS = 8
N_KV_HEADS = 2
GQA_GROUP = N_Q_HEADS // N_KV_HEADS
ATTN_WIDTH = N_Q_HEADS * HEAD_DIM
KV_WIDTH = N_KV_HEADS * HEAD_DIM
WINDOW = 128
Q_BLOCK = 128
ROPE_THETA = 500000.0
ROT_DIM = HEAD_DIM // 4
D_FF = 4 * D_MODEL
IN_PROJ_WIDTH = 2 * GMLP_WIDTH + ATTN_WIDTH + 2 * KV_WIDTH
N_MOD = 6
EPS = 1e-5

LANES = 128
TOKENS_PER_STEP = 512
FF_CHUNK = 1024
ADA_COLS = 1536
VMEM_LIMIT_BYTES = 56 * 1024 * 1024

F32 = jnp.float32
BF16 = jnp.bfloat16


def _ada_kernel(c_ref, w_ref, b_ref, o_ref):
    c = c_ref[...]
    c_act = c * (1.0 / (1.0 + jnp.exp(-c)))
    o_ref[...] = jnp.dot(c_act, w_ref[...], preferred_element_type=F32) + b_ref[...]


def _rms_scale(x):
    return lax.rsqrt(jnp.mean(x * x, axis=-1, keepdims=True) + EPS)


def _gelu_tanh(x):
    c = math.sqrt(2.0 / math.pi)
    return x * (0.5 * (1.0 + jnp.tanh(c * (x + 0.044715 * (x * x * x)))))


def _mixer_kernel(sinks_ref, x_ref, pos_ref, mod_ref, g_ref, win_ref, wcat_ref, sb_ref,
                  freq_ref, sga_ref, sgb_ref, wout_ref, o_ref, k_scr, v_scr, mix_scr):
    ts = TOKENS_PER_STEP
    s = pl.program_id(1)
    x = x_ref[0]
    shift1 = mod_ref[0, 0:1, :]
    scale1 = mod_ref[0, 1:2, :]
    gate1 = mod_ref[0, 2:3, :]

    h = (x * _rms_scale(x)) * (g_ref[...] * (1.0 + scale1)) + shift1
    proj = jnp.dot(h.astype(BF16), win_ref[...], preferred_element_type=F32)

    lane = lax.broadcasted_iota(jnp.int32, (1, LANES), 1)
    left = lane < HEAD_DIM

    t_idx = lax.broadcasted_iota(jnp.int32, (CHUNK, 2 * CHUNK), 0)
    s_idx = lax.broadcasted_iota(jnp.int32, (CHUNK, 2 * CHUNK), 1)
    causal = (s_idx & (CHUNK - 1)) <= t_idx
    w_pairs = [jnp.where(causal, wcat_ref[p], 0.0).astype(BF16) for p in range(GMLP_GROUPS // 2)]
    for c in range(ts // CHUNK):
        rows = slice(c * CHUNK, (c + 1) * CHUNK)
        u = _gelu_tanh(proj[rows, 0:GMLP_WIDTH])
        v = _gelu_tanh(proj[rows, GMLP_WIDTH:2 * GMLP_WIDTH])
        sv = []
        for p in range(GMLP_GROUPS // 2):
            vp = v[:, p * LANES:(p + 1) * LANES]
            bd = jnp.concatenate([jnp.where(left, vp, 0.0), jnp.where(left, 0.0, vp)], axis=0)
            sv.append(jnp.dot(w_pairs[p], bd.astype(BF16), preferred_element_type=F32))
        sv = jnp.concatenate(sv, axis=1) + sb_ref[...]
        mix_scr[rows, 0:GMLP_WIDTH] = (u * sv).astype(BF16)

    ang = pos_ref[0].astype(F32) * freq_ref[...]
    cos_t = jnp.cos(ang)
    sin_t = jnp.sin(ang)
    sin_a = sin_t * sga_ref[...]
    sin_b = sin_t * sgb_ref[...]

    def rope(t):
        return (t * cos_t + pltpu.roll(t, LANES - ROT_DIM // 2, 1) * sin_a
                + pltpu.roll(t, ROT_DIM // 2, 1) * sin_b)

    o = 2 * GMLP_WIDTH
    scale = 1.0 / math.sqrt(HEAD_DIM)
    q = [(rope(proj[:, o + i * LANES:o + (i + 1) * LANES]) * scale).astype(BF16)
         for i in range(ATTN_WIDTH // LANES)]
    o += ATTN_WIDTH
    k = rope(proj[:, o:o + KV_WIDTH])
    v_att = proj[:, o + KV_WIDTH:o + 2 * KV_WIDTH]

    @pl.when(s == 0)
    def _():
        k_scr[:, 0:Q_BLOCK, :] = jnp.zeros((N_KV_HEADS, Q_BLOCK, LANES), BF16)
        v_scr[:, 0:Q_BLOCK, :] = jnp.zeros((N_KV_HEADS, Q_BLOCK, LANES), BF16)

    k_sw = pltpu.roll(k, HEAD_DIM, 1)
    v_sw = pltpu.roll(v_att, HEAD_DIM, 1)
    k_scr[0, Q_BLOCK:Q_BLOCK + ts, :] = jnp.where(left, k, k_sw).astype(BF16)
    k_scr[1, Q_BLOCK:Q_BLOCK + ts, :] = jnp.where(left, k_sw, k).astype(BF16)
    v_scr[0, Q_BLOCK:Q_BLOCK + ts, :] = jnp.where(left, v_att, v_sw).astype(BF16)
    v_scr[1, Q_BLOCK:Q_BLOCK + ts, :] = jnp.where(left, v_sw, v_att).astype(BF16)

    qi = lax.broadcasted_iota(jnp.int32, (Q_BLOCK, 2 * Q_BLOCK), 0)
    kj = lax.broadcasted_iota(jnp.int32, (Q_BLOCK, 2 * Q_BLOCK), 1)
    band = (kj > qi) & (kj <= qi + WINDOW)
    bias_band = jnp.where(band, 0.0, -jnp.inf).astype(F32)
    bias_first = jnp.where(band & (kj >= Q_BLOCK), 0.0, -jnp.inf).astype(F32)

    for j in range(ts // Q_BLOCK):
        rows = slice(j * Q_BLOCK, (j + 1) * Q_BLOCK)
        bias = jnp.where(s == 0, bias_first, bias_band) if j == 0 else bias_band
        for hk in range(N_KV_HEADS):
            k_win = k_scr[hk, j * Q_BLOCK:(j + 2) * Q_BLOCK, :]
            v_win = v_scr[hk, j * Q_BLOCK:(j + 2) * Q_BLOCK, :]
            zero = jnp.zeros((), BF16)
            lhs = []
            sink_rows = []
            for m in range(2):
                qp = q[2 * hk + m][rows, :]
                lhs.append(jnp.where(left, qp, zero))
                lhs.append(jnp.where(left, zero, qp))
                for e in range(2):
                    head = hk * GQA_GROUP + 2 * m + e
                    sink_rows.append(jnp.full((Q_BLOCK, 1), sinks_ref[head], F32))
            lhs = jnp.concatenate(lhs, axis=0)
            sink = jnp.concatenate(sink_rows, axis=0)
            sc = lax.dot_general(lhs, k_win, (((1,), (1,)), ((), ())),
                                 preferred_element_type=F32)
            sc = (sc.reshape(GQA_GROUP, Q_BLOCK, 2 * Q_BLOCK) + bias[None]).reshape(
                GQA_GROUP * Q_BLOCK, 2 * Q_BLOCK)
            mx = jnp.maximum(jnp.max(sc, axis=-1, keepdims=True), sink)
            p = jnp.exp(sc - mx)
            denom = jnp.sum(p, axis=-1, keepdims=True) + jnp.exp(sink - mx)
            pv = jnp.dot(p.astype(BF16), v_win, preferred_element_type=F32)
            pv = pv * (1.0 / denom)
            for m in range(2):
                pair = jnp.where(left, pv[(2 * m) * Q_BLOCK:(2 * m + 1) * Q_BLOCK],
                                 pv[(2 * m + 1) * Q_BLOCK:(2 * m + 2) * Q_BLOCK])
                col = GMLP_WIDTH + (2 * hk + m) * LANES
                mix_scr[rows, col:col + LANES] = pair.astype(BF16)

    k_scr[:, 0:Q_BLOCK, :] = k_scr[:, ts:ts + Q_BLOCK, :]
    v_scr[:, 0:Q_BLOCK, :] = v_scr[:, ts:ts + Q_BLOCK, :]

    mix = jnp.dot(mix_scr[...], wout_ref[...], preferred_element_type=F32)
    o_ref[0] = x + gate1 * mix


def _ffn_kernel(x_ref, mod_ref, g_ref, w1_ref, w2_ref, gf_ref, o_ref, acc_scr):
    x = x_ref[0]
    shift2 = mod_ref[0, 3:4, :]
    scale2 = mod_ref[0, 4:5, :]
    gate2 = mod_ref[0, 5:6, :]
    h = ((x * _rms_scale(x)) * (g_ref[...] * (1.0 + scale2)) + shift2).astype(BF16)

    def body(c, carry):
        a = jnp.dot(h, w1_ref[c], preferred_element_type=F32)
        a = jnp.maximum(a, 0.0)
        a = (a * a).astype(BF16)
        acc_scr[...] += jnp.dot(a, w2_ref[c], preferred_element_type=F32)
        return carry

    acc_scr[...] = jnp.zeros_like(acc_scr)
    lax.fori_loop(0, D_FF // FF_CHUNK, body, 0)
    y = x + gate2 * acc_scr[...]
    o_ref[0] = (y * _rms_scale(y)) * gf_ref[...]


def _rope_rows():
    lane = np.arange(LANES)
    in_head = lane % HEAD_DIM
    half = ROT_DIM // 2
    inv_freq = ROPE_THETA ** (-np.arange(0, ROT_DIM, 2, dtype=np.float32) / ROT_DIM)
    freq = np.where(in_head < ROT_DIM, inv_freq[in_head % half], 0.0).astype(np.float32)
    sga = np.where(in_head < half, -1.0, 0.0).astype(np.float32)
    sgb = np.where((in_head >= half) & (in_head < ROT_DIM), 1.0, 0.0).astype(np.float32)
    return freq[None], sga[None], sgb[None]


def _const_spec(shape):
    zeros = (0,) * len(shape)
    return pl.BlockSpec(shape, lambda *_: zeros)


def kernel(x, c, positions, w_ada, b_ada, g_mix, w_in, w_spatial, b_spatial, sinks, w_out, g_ffn,
           w_ff1, w_ff2, g_final):
    B, S, D = x.shape
    ts = TOKENS_PER_STEP
    assert D == D_MODEL and S % ts == 0 and w_ada.shape[0] == 1

    n_ada = N_MOD * D
    mod = pl.pallas_call(
        _ada_kernel,
        grid=(n_ada // ADA_COLS,),
        in_specs=[pl.BlockSpec((B, D), lambda i: (0, 0)),
                  pl.BlockSpec((D, ADA_COLS), lambda i: (0, i)),
                  pl.BlockSpec((1, ADA_COLS), lambda i: (0, i))],
        out_specs=pl.BlockSpec((B, ADA_COLS), lambda i: (0, i)),
        out_shape=jax.ShapeDtypeStruct((B, n_ada), F32),
        compiler_params=pltpu.CompilerParams(dimension_semantics=("arbitrary",),
                                             vmem_limit_bytes=VMEM_LIMIT_BYTES),
        name="ada_mod",
    )(c, w_ada[0], b_ada[0][None])
    mod = mod.reshape(B, N_MOD, D)

    ws = w_spatial[0]
    wcat = ws.reshape(GMLP_GROUPS // 2, 2, CHUNK, CHUNK).transpose(0, 2, 1, 3).reshape(
        GMLP_GROUPS // 2, CHUNK, 2 * CHUNK)
    sbias = jnp.repeat(b_spatial[0].T, HEAD_DIM, axis=1)
    freq, sga, sgb = _rope_rows()
    pos3 = positions.reshape(B, S, 1)

    grid = (B, S // ts)
    tok_spec = pl.BlockSpec((1, ts, D), lambda b, s: (b, s, 0))
    mod_spec = pl.BlockSpec((1, N_MOD, D), lambda b, s: (b, 0, 0))
    params = pltpu.CompilerParams(dimension_semantics=("arbitrary", "arbitrary"),
                                  vmem_limit_bytes=VMEM_LIMIT_BYTES)

    x1 = pl.pallas_call(
        _mixer_kernel,
        grid=grid,
        in_specs=[pl.BlockSpec(memory_space=pltpu.SMEM),
                  tok_spec,
                  pl.BlockSpec((1, ts, 1), lambda b, s: (b, s, 0)),
                  mod_spec,
                  _const_spec((1, D)),
                  _const_spec((D, IN_PROJ_WIDTH)),
                  _const_spec((GMLP_GROUPS // 2, CHUNK, 2 * CHUNK)),
                  _const_spec((CHUNK, GMLP_WIDTH)),
                  _const_spec((1, LANES)), _const_spec((1, LANES)), _const_spec((1, LANES)),
                  _const_spec((D, D))],
        out_specs=tok_spec,
        out_shape=jax.ShapeDtypeStruct((B, S, D), F32),
        scratch_shapes=[pltpu.VMEM((N_KV_HEADS, Q_BLOCK + ts, LANES), BF16),
                        pltpu.VMEM((N_KV_HEADS, Q_BLOCK + ts, LANES), BF16),
                        pltpu.VMEM((ts, D), BF16)],
        compiler_params=params,
        name="token_mixer",
    )(sinks[0], x, pos3, mod, g_mix, w_in[0].astype(BF16), wcat, sbias,
      jnp.asarray(freq), jnp.asarray(sga), jnp.asarray(sgb), w_out[0].astype(BF16))

    n_ff = D_FF // FF_CHUNK
    w1 = w_ff1[0].astype(BF16).reshape(D, n_ff, FF_CHUNK).transpose(1, 0, 2)
    w2 = w_ff2[0].astype(BF16).reshape(n_ff, FF_CHUNK, D)
    return pl.pallas_call(
        _ffn_kernel,
        grid=grid,
        in_specs=[tok_spec, mod_spec, _const_spec((1, D)),
                  _const_spec((n_ff, D, FF_CHUNK)), _const_spec((n_ff, FF_CHUNK, D)),
                  _const_spec((1, D))],
        out_specs=tok_spec,
        out_shape=jax.ShapeDtypeStruct((B, S, D), F32),
        scratch_shapes=[pltpu.VMEM((ts, D), F32)],
        compiler_params=params,
        name="channel_mixer",
    )(x1, mod, g_ffn, w1, w2, g_final[None])
```

```python
import functools
import math

import numpy as np
import jax
import jax.numpy as jnp
from jax import lax
from jax.experimental import pallas as pl
from jax.experimental.pallas import tpu as pltpu

D_MODEL = 1024
HEAD_DIM = 64
GMLP_GROUPS = 8
GMLP_WIDTH = GMLP_GROUPS * HEAD_DIM
CHUNK = 128
N_Q_HEADS = 8
N_KV_HEADS = 2
GQA_GROUP = N_Q_HEADS // N_KV_HEADS
ATTN_WIDTH = N_Q_HEADS * HEAD_DIM
KV_WIDTH = N_KV_HEADS * HEAD_DIM
WINDOW = 128
Q_BLOCK = 128
ROPE_THETA = 500000.0
ROT_DIM = HEAD_DIM // 4
D_FF = 4 * D_MODEL
IN_PROJ_WIDTH = 2 * GMLP_WIDTH + ATTN_WIDTH + 2 * KV_WIDTH
N_MOD = 6
EPS = 1e-5
LOG2E = math.log2(math.e)

LANES = 128
TOKENS_PER_STEP = 512
FF_CHUNK = 1024
ADA_COLS = 1536
VMEM_LIMIT_BYTES = 56 * 1024 * 1024

F32 = jnp.float32
BF16 = jnp.bfloat16


def _ada_kernel(c_ref, w_ref, b_ref, o_ref):
    c = c_ref[...]
    c_act = c * (1.0 / (1.0 + jnp.exp(-c)))
    o_ref[...] = jnp.dot(c_act, w_ref[...], preferred_element_type=F32) + b_ref[...]


def _rms_scale(x):
    return lax.rsqrt(jnp.mean(x * x, axis=-1, keepdims=True) + EPS)


def _gelu_tanh(x):
    c = 2.0 * math.sqrt(2.0 / math.pi) * LOG2E
    return x / (1.0 + jnp.exp2(x * (-c - (c * 0.044715) * (x * x))))


def _mixer_kernel(sinks_ref, x_ref, pos_ref, mod_ref, g_ref, win_ref, wcat_ref, sb_ref,
                  invf_ref, wout_ref, o_ref, k_scr, v_scr, mix_scr):
    ts = TOKENS_PER_STEP
    s = pl.program_id(1)
    x = x_ref[0]
    shift1 = mod_ref[0, 0:1, :]
    scale1 = mod_ref[0, 1:2, :]
    gate1 = mod_ref[0, 2:3, :]

    h = (x * _rms_scale(x)) * (g_ref[...] * (1.0 + scale1)) + shift1
    proj = jnp.dot(h.astype(BF16), win_ref[...], preferred_element_type=F32)

    lane = lax.broadcasted_iota(jnp.int32, (1, LANES), 1)
    left = lane < HEAD_DIM

    t_idx = lax.broadcasted_iota(jnp.int32, (CHUNK, 2 * CHUNK), 0)
    s_idx = lax.broadcasted_iota(jnp.int32, (CHUNK, 2 * CHUNK), 1)
    causal = (s_idx & (CHUNK - 1)) <= t_idx
    w_pairs = [jnp.where(causal, wcat_ref[p], 0.0).astype(BF16) for p in range(GMLP_GROUPS // 2)]
    for c in range(ts // CHUNK):
        rows = slice(c * CHUNK, (c + 1) * CHUNK)
        u = _gelu_tanh(proj[rows, 0:GMLP_WIDTH])
        v = _gelu_tanh(proj[rows, GMLP_WIDTH:2 * GMLP_WIDTH])
        sv = []
        for p in range(GMLP_GROUPS // 2):
            vp = v[:, p * LANES:(p + 1) * LANES]
            bd = jnp.concatenate([jnp.where(left, vp, 0.0), jnp.where(left, 0.0, vp)], axis=0)
            sv.append(jnp.dot(w_pairs[p], bd.astype(BF16), preferred_element_type=F32))
        sv = jnp.concatenate(sv, axis=1) + sb_ref[...]
        mix_scr[rows, 0:GMLP_WIDTH] = (u * sv).astype(BF16)

    ang = invf_ref[...] * pos_ref[0].astype(F32)
    cos8 = jnp.cos(ang)
    sin8 = jnp.sin(ang)
    one8 = jnp.ones_like(ang)
    zero8 = jnp.zeros_like(ang)
    tiles_per_head = HEAD_DIM // 8
    heads_per_vreg = LANES // HEAD_DIM
    cos_t = jnp.concatenate(([cos8, cos8] + [one8] * (tiles_per_head - 2)) * heads_per_vreg, axis=0).T
    sin_a = jnp.concatenate(([-sin8] + [zero8] * (tiles_per_head - 1)) * heads_per_vreg, axis=0).T
    sin_b = jnp.concatenate(([zero8, sin8] + [zero8] * (tiles_per_head - 2)) * heads_per_vreg, axis=0).T

    def rope(t):
        return (t * cos_t + pltpu.roll(t, LANES - ROT_DIM // 2, 1) * sin_a
                + pltpu.roll(t, ROT_DIM // 2, 1) * sin_b)

    o = 2 * GMLP_WIDTH
    scale = LOG2E / math.sqrt(HEAD_DIM)
    q = [(rope(proj[:, o + i * LANES:o + (i + 1) * LANES]) * scale).astype(BF16)
         for i in range(ATTN_WIDTH // LANES)]
    o += ATTN_WIDTH
    k = rope(proj[:, o:o + KV_WIDTH])
    v_att = proj[:, o + KV_WIDTH:o + 2 * KV_WIDTH]

    @pl.when(s == 0)
    def _():
        k_scr[:, 0:Q_BLOCK, :] = jnp.zeros((N_KV_HEADS, Q_BLOCK, LANES), BF16)
        v_scr[:, 0:Q_BLOCK, :] = jnp.zeros((N_KV_HEADS, Q_BLOCK, LANES), BF16)

    k_sw = pltpu.roll(k, HEAD_DIM, 1)
    v_sw = pltpu.roll(v_att, HEAD_DIM, 1)
    k_scr[0, Q_BLOCK:Q_BLOCK + ts, :] = jnp.where(left, k, k_sw).astype(BF16)
    k_scr[1, Q_BLOCK:Q_BLOCK + ts, :] = jnp.where(left, k_sw, k).astype(BF16)
    v_scr[0, Q_BLOCK:Q_BLOCK + ts, :] = jnp.where(left, v_att, v_sw).astype(BF16)
    v_scr[1, Q_BLOCK:Q_BLOCK + ts, :] = jnp.where(left, v_sw, v_att).astype(BF16)

    qi = lax.broadcasted_iota(jnp.int32, (Q_BLOCK, 2 * Q_BLOCK), 0)
    kj = lax.broadcasted_iota(jnp.int32, (Q_BLOCK, 2 * Q_BLOCK), 1)
    band = (kj > qi) & (kj <= qi + WINDOW)
    bias_band = jnp.where(band, 0.0, -jnp.inf).astype(F32)
    bias_first = jnp.where(band & (kj >= Q_BLOCK), 0.0, -jnp.inf).astype(F32)

    for j in range(ts // Q_BLOCK):
        rows = slice(j * Q_BLOCK, (j + 1) * Q_BLOCK)
        bias = jnp.where(s == 0, bias_first, bias_band) if j == 0 else bias_band
        for hk in range(N_KV_HEADS):
            k_win = k_scr[hk, j * Q_BLOCK:(j + 2) * Q_BLOCK, :]
            v_win = v_scr[hk, j * Q_BLOCK:(j + 2) * Q_BLOCK, :]
            zero = jnp.zeros((), BF16)
            lhs = []
            sink_rows = []
            for m in range(2):
                qp = q[2 * hk + m][rows, :]
                lhs.append(jnp.where(left, qp, zero))
                lhs.append(jnp.where(left, zero, qp))
                for e in range(2):
                    head = hk * GQA_GROUP + 2 * m + e
                    sink_rows.append(jnp.full((Q_BLOCK, 1), sinks_ref[head] * LOG2E, F32))
            lhs = jnp.concatenate(lhs, axis=0)
            sink = jnp.concatenate(sink_rows, axis=0)
            sc = lax.dot_general(lhs, k_win, (((1,), (1,)), ((), ())),
                                 preferred_element_type=F32)
            sc = (sc.reshape(GQA_GROUP, Q_BLOCK, 2 * Q_BLOCK) + bias[None]).reshape(
                GQA_GROUP * Q_BLOCK, 2 * Q_BLOCK)
            mx = jnp.maximum(jnp.max(sc, axis=-1, keepdims=True), sink)
            p = jnp.exp2(sc - mx)
            denom = jnp.sum(p, axis=-1, keepdims=True) + jnp.exp2(sink - mx)
            pv = jnp.dot(p.astype(BF16), v_win, preferred_element_type=F32)
            pv = pv * (1.0 / denom)
            for m in range(2):
                pair = jnp.where(left, pv[(2 * m) * Q_BLOCK:(2 * m + 1) * Q_BLOCK],
                                 pv[(2 * m + 1) * Q_BLOCK:(2 * m + 2) * Q_BLOCK])
                col = GMLP_WIDTH + (2 * hk + m) * LANES
                mix_scr[rows, col:col + LANES] = pair.astype(BF16)

    k_scr[:, 0:Q_BLOCK, :] = k_scr[:, ts:ts + Q_BLOCK, :]
    v_scr[:, 0:Q_BLOCK, :] = v_scr[:, ts:ts + Q_BLOCK, :]

    mix = jnp.dot(mix_scr[...], wout_ref[...], preferred_element_type=F32)
    o_ref[0] = x + gate1 * mix


def _ffn_kernel(x_ref, mod_ref, g_ref, w1_ref, w2_ref, gf_ref, o_ref, acc_scr):
    x = x_ref[0]
    shift2 = mod_ref[0, 3:4, :]
    scale2 = mod_ref[0, 4:5, :]
    gate2 = mod_ref[0, 5:6, :]
    h = ((x * _rms_scale(x)) * (g_ref[...] * (1.0 + scale2)) + shift2).astype(BF16)

    for c in range(D_FF // FF_CHUNK):
        cols = slice(c * FF_CHUNK, (c + 1) * FF_CHUNK)
        a = jnp.dot(h, w1_ref[:, cols], preferred_element_type=F32)
        a = jnp.maximum(a, 0.0)
        a = (a * a).astype(BF16)
        part = jnp.dot(a, w2_ref[cols, :], preferred_element_type=F32)
        if c == 0:
            acc_scr[...] = part
        else:
            acc_scr[...] += part
    y = x + gate2 * acc_scr[...]
    o_ref[0] = (y * _rms_scale(y)) * gf_ref[...]


def _inv_freq_column():
    inv_freq = ROPE_THETA ** (-np.arange(0, ROT_DIM, 2, dtype=np.float32) / ROT_DIM)
    return inv_freq.astype(np.float32)[:, None]


def _const_spec(shape):
    zeros = (0,) * len(shape)
    return pl.BlockSpec(shape, lambda *_: zeros)


def kernel(x, c, positions, w_ada, b_ada, g_mix, w_in, w_spatial, b_spatial, sinks, w_out, g_ffn,
           w_ff1, w_ff2, g_final):
    B, S, D = x.shape
    ts = TOKENS_PER_STEP
    assert D == D_MODEL and S % ts == 0 and w_ada.shape[0] == 1

    n_ada = N_MOD * D
    mod = pl.pallas_call(
        _ada_kernel,
        grid=(n_ada // ADA_COLS,),
        in_specs=[pl.BlockSpec((B, D), lambda i: (0, 0)),
                  pl.BlockSpec((D, ADA_COLS), lambda i: (0, i)),
                  pl.BlockSpec((1, ADA_COLS), lambda i: (0, i))],
        out_specs=pl.BlockSpec((B, ADA_COLS), lambda i: (0, i)),
        out_shape=jax.ShapeDtypeStruct((B, n_ada), F32),
        compiler_params=pltpu.CompilerParams(dimension_semantics=("arbitrary",),
                                             vmem_limit_bytes=VMEM_LIMIT_BYTES),
        name="ada_mod",
    )(c, w_ada[0], b_ada[0][None])
    mod = mod.reshape(B, N_MOD, D)

    ws = w_spatial[0]
    wcat = ws.reshape(GMLP_GROUPS // 2, 2, CHUNK, CHUNK).transpose(0, 2, 1, 3).reshape(
        GMLP_GROUPS // 2, CHUNK, 2 * CHUNK)
    sbias = jnp.repeat(b_spatial[0].T, HEAD_DIM, axis=1)
    pos3 = positions.reshape(B, 1, S)

    grid = (B, S // ts)
    tok_spec = pl.BlockSpec((1, ts, D), lambda b, s: (b, s, 0))
    mod_spec = pl.BlockSpec((1, N_MOD, D), lambda b, s: (b, 0, 0))
    params = pltpu.CompilerParams(dimension_semantics=("arbitrary", "arbitrary"),
                                  vmem_limit_bytes=VMEM_LIMIT_BYTES)

    x1 = pl.pallas_call(
        _mixer_kernel,
        grid=grid,
        in_specs=[pl.BlockSpec(memory_space=pltpu.SMEM),
                  tok_spec,
                  pl.BlockSpec((1, 1, ts), lambda b, s: (b, 0, s)),
                  mod_spec,
                  _const_spec((1, D)),
                  _const_spec((D, IN_PROJ_WIDTH)),
                  _const_spec((GMLP_GROUPS // 2, CHUNK, 2 * CHUNK)),
                  _const_spec((CHUNK, GMLP_WIDTH)),
                  _const_spec((ROT_DIM // 2, 1)),
                  _const_spec((D, D))],
        out_specs=tok_spec,
        out_shape=jax.ShapeDtypeStruct((B, S, D), F32),
        scratch_shapes=[pltpu.VMEM((N_KV_HEADS, Q_BLOCK + ts, LANES), BF16),
                        pltpu.VMEM((N_KV_HEADS, Q_BLOCK + ts, LANES), BF16),
                        pltpu.VMEM((ts, D), BF16)],
        compiler_params=params,
        name="token_mixer",
    )(sinks[0], x, pos3, mod, g_mix, w_in[0].astype(BF16), wcat, sbias,
      jnp.asarray(_inv_freq_column()), w_out[0].astype(BF16))

    w1 = w_ff1[0].astype(BF16)
    w2 = w_ff2[0].astype(BF16)
    return pl.pallas_call(
        _ffn_kernel,
        grid=grid,
        in_specs=[tok_spec, mod_spec, _const_spec((1, D)),
                  _const_spec((D, D_FF)), _const_spec((D_FF, D)),
                  _const_spec((1, D))],
        out_specs=tok_spec,
        out_shape=jax.ShapeDtypeStruct((B, S, D), F32),
        scratch_shapes=[pltpu.VMEM((ts, D), F32)],
        compiler_params=params,
        name="channel_mixer",
    )(x1, mod, g_ffn, w1, w2, g_final[None])
```

```python
import functools
import math

import numpy as np
import jax
import jax.numpy as jnp
from jax import lax
from jax.experimental import pallas as pl
from jax.experimental.pallas import tpu as pltpu

D_MODEL = 1024
HEAD_DIM = 64
GMLP_GROUPS = 8
GMLP_WIDTH = GMLP_GROUPS * HEAD_DIM
CHUNK = 128
N_Q_HEADS = 8
N_KV_HEADS = 2
GQA_GROUP = N_Q_HEADS // N_KV_HEADS
ATTN_WIDTH = N_Q_HEADS * HEAD_DIM
KV_WIDTH = N_KV_HEADS * HEAD_DIM
WINDOW = 128
Q_BLOCK = 128
ROPE_THETA = 500000.0
ROT_DIM = HEAD_DIM // 4
D_FF = 4 * D_MODEL
IN_PROJ_WIDTH = 2 * GMLP_WIDTH + ATTN_WIDTH + 2 * KV_WIDTH
N_MOD = 6
EPS = 1e-5
LOG2E = math.log2(math.e)

LANES = 128
SUBLANES = 8
TOKENS_PER_STEP = 512
FF_CHUNK = 1024
ADA_COLS = 1536
VMEM_LIMIT_BYTES = 60 * 1024 * 1024

N_SUB = TOKENS_PER_STEP // Q_BLOCK
N_ATT = N_SUB * N_KV_HEADS
N_PAIRS = GMLP_GROUPS // 2

F32 = jnp.float32
BF16 = jnp.bfloat16


def _ada_kernel(c_ref, w_ref, b_ref, o_ref):
    c = c_ref[...]
    c_act = c * (1.0 / (1.0 + jnp.exp(-c)))
    o_ref[...] = jnp.dot(c_act, w_ref[...], preferred_element_type=F32) + b_ref[...]


def _rms_scale(x):
    return lax.rsqrt(jnp.mean(x * x, axis=-1, keepdims=True) + EPS)


def _gelu_tanh(x):
    c = 2.0 * math.sqrt(2.0 / math.pi) * LOG2E
    return x / (1.0 + jnp.exp2(x * (-c - (c * 0.044715) * (x * x))))


def _mixer_stream(s, is_last_in_seq, mod, sinks_ref, x_ref, pos_ref, g_ref, win_ref, wcat_ref, sb_ref,
                  invf_ref, mix_scr, k_scr, v_scr, bd_scr, lhs_scr, sv_scr, sc_scr, p_scr, inv_scr):
    ts = TOKENS_PER_STEP
    x = x_ref[0]
    shift1 = mod[0:1, :]
    scale1 = mod[1:2, :]

    h = ((x * _rms_scale(x)) * (g_ref[...] * (1.0 + scale1)) + shift1).astype(BF16)
    yield
    proj = jnp.dot(h, win_ref[...], preferred_element_type=F32)
    yield

    lane = lax.broadcasted_iota(jnp.int32, (1, LANES), 1)
    left = lane < HEAD_DIM

    t_idx = lax.broadcasted_iota(jnp.int32, (CHUNK, 2 * CHUNK), 0)
    s_idx = lax.broadcasted_iota(jnp.int32, (CHUNK, 2 * CHUNK), 1)
    causal = (s_idx & (CHUNK - 1)) <= t_idx
    w_pairs = [jnp.where(causal, wcat_ref[p], 0.0).astype(BF16) for p in range(N_PAIRS)]
    for c in range(ts // CHUNK):
        rows = slice(c * CHUNK, (c + 1) * CHUNK)
        v = _gelu_tanh(proj[rows, GMLP_WIDTH:2 * GMLP_WIDTH])
        for p in range(N_PAIRS):
            vp = v[:, p * LANES:(p + 1) * LANES]
            bd_scr[c * N_PAIRS + p, 0:CHUNK, :] = jnp.where(left, vp, 0.0).astype(BF16)
            bd_scr[c * N_PAIRS + p, CHUNK:2 * CHUNK, :] = jnp.where(left, 0.0, vp).astype(BF16)
        yield

    ang = invf_ref[...] * pos_ref[0].astype(F32)
    cos8 = jnp.cos(ang)
    sin8 = jnp.sin(ang)
    one8 = jnp.ones_like(ang)
    zero8 = jnp.zeros_like(ang)
    tiles_per_head = HEAD_DIM // SUBLANES
    heads_per_vreg = LANES // HEAD_DIM
    cos_t = jnp.concatenate(([cos8, cos8] + [one8] * (tiles_per_head - 2)) * heads_per_vreg, axis=0).T
    sin_a = jnp.concatenate(([-sin8] + [zero8] * (tiles_per_head - 1)) * heads_per_vreg, axis=0).T
    sin_b = jnp.concatenate(([zero8, sin8] + [zero8] * (tiles_per_head - 2)) * heads_per_vreg, axis=0).T

    def rope(t):
        return (t * cos_t + pltpu.roll(t, LANES - ROT_DIM // 2, 1) * sin_a
                + pltpu.roll(t, ROT_DIM // 2, 1) * sin_b)

    o = 2 * GMLP_WIDTH
    scale = LOG2E / math.sqrt(HEAD_DIM)
    zero = jnp.zeros((), BF16)
    for i in range(ATTN_WIDTH // LANES):
        qp = (rope(proj[:, o + i * LANES:o + (i + 1) * LANES]) * scale).astype(BF16)
        hk, m = divmod(i, 2)
        halves = (jnp.where(left, qp, zero), jnp.where(left, zero, qp))
        for j in range(N_SUB):
            for e in range(2):
                g = 2 * m + e
                lhs_scr[j * N_KV_HEADS + hk, g * Q_BLOCK:(g + 1) * Q_BLOCK, :] = (
                    halves[e][j * Q_BLOCK:(j + 1) * Q_BLOCK])
    yield
    o += ATTN_WIDTH
    k = rope(proj[:, o:o + KV_WIDTH])
    v_att = proj[:, o + KV_WIDTH:o + 2 * KV_WIDTH]
    k_sw = pltpu.roll(k, HEAD_DIM, 1)
    k_scr[0, Q_BLOCK:Q_BLOCK + ts, :] = jnp.where(left, k, k_sw).astype(BF16)
    k_scr[1, Q_BLOCK:Q_BLOCK + ts, :] = jnp.where(left, k_sw, k).astype(BF16)
    v_scr[:, Q_BLOCK:Q_BLOCK + ts] = v_att.T.astype(BF16)
    yield

    for c in range(ts // CHUNK):
        rows = slice(c * CHUNK, (c + 1) * CHUNK)
        for p in range(N_PAIRS):
            sv_scr[rows, p * LANES:(p + 1) * LANES] = jnp.dot(
                w_pairs[p], bd_scr[c * N_PAIRS + p], preferred_element_type=F32)
    for j in range(N_SUB):
        for hk in range(N_KV_HEADS):
            n = j * N_KV_HEADS + hk
            k_win = k_scr[hk, j * Q_BLOCK:(j + 2) * Q_BLOCK, :]
            sc_scr[n] = lax.dot_general(k_win, lhs_scr[n], (((1,), (1,)), ((), ())),
                                        preferred_element_type=F32)
    yield

    for c in range(ts // CHUNK):
        rows = slice(c * CHUNK, (c + 1) * CHUNK)
        u = _gelu_tanh(proj[rows, 0:GMLP_WIDTH])
        mix_scr[rows, 0:GMLP_WIDTH] = (u * (sv_scr[rows, :] + sb_ref[...])).astype(BF16)
        yield

    kj = lax.broadcasted_iota(jnp.int32, (2 * Q_BLOCK, Q_BLOCK), 0)
    qi = lax.broadcasted_iota(jnp.int32, (2 * Q_BLOCK, Q_BLOCK), 1)
    band = (kj > qi) & (kj <= qi + WINDOW)
    bias_band = jnp.where(band, 0.0, -jnp.inf).astype(F32)
    bias_first = jnp.where(band & (kj >= Q_BLOCK), 0.0, -jnp.inf).astype(F32)
    for j in range(N_SUB):
        bias = jnp.where(s == 0, bias_first, bias_band) if j == 0 else bias_band
        for hk in range(N_KV_HEADS):
            n = j * N_KV_HEADS + hk
            sink = jnp.concatenate(
                [jnp.full((1, Q_BLOCK), sinks_ref[hk * GQA_GROUP + g] * LOG2E, F32)
                 for g in range(GQA_GROUP)], axis=1)
            sc = jnp.concatenate(
                [sc_scr[n, :, g * Q_BLOCK:(g + 1) * Q_BLOCK] + bias for g in range(GQA_GROUP)], axis=1)
            mx = jnp.maximum(jnp.max(sc, axis=0, keepdims=True), sink)
            p = jnp.exp2(sc - mx)
            denom = jnp.sum(p, axis=0, keepdims=True) + jnp.exp2(sink - mx)
            p_scr[n] = p.astype(BF16)
            inv_scr[n] = 1.0 / denom
            yield

    for j in range(N_SUB):
        rows = slice(j * Q_BLOCK, (j + 1) * Q_BLOCK)
        for hk in range(N_KV_HEADS):
            n = j * N_KV_HEADS + hk
            v_win = v_scr[hk * HEAD_DIM:(hk + 1) * HEAD_DIM, j * Q_BLOCK:(j + 2) * Q_BLOCK]
            pv = jnp.dot(v_win, p_scr[n], preferred_element_type=F32) * inv_scr[n]
            for m in range(2):
                pair = jnp.concatenate([pv[:, (2 * m) * Q_BLOCK:(2 * m + 1) * Q_BLOCK],
                                        pv[:, (2 * m + 1) * Q_BLOCK:(2 * m + 2) * Q_BLOCK]], axis=0)
                col = GMLP_WIDTH + (2 * hk + m) * LANES
                mix_scr[rows, col:col + LANES] = pair.T.astype(BF16)

    k_last = k_scr[:, ts:ts + Q_BLOCK, :]
    v_last = v_scr[:, ts:ts + Q_BLOCK]
    k_scr[:, 0:Q_BLOCK, :] = jnp.where(is_last_in_seq, jnp.zeros_like(k_last), k_last)
    v_scr[:, 0:Q_BLOCK] = jnp.where(is_last_in_seq, jnp.zeros_like(v_last), v_last)


def _ffn_stream(mod, x_ref, mix_scr, wout_ref, g_ref, w1_ref, w2_ref, gf_ref, o_ref, acc_scr):
    gate1 = mod[2:3, :]
    shift2 = mod[3:4, :]
    scale2 = mod[4:5, :]
    gate2 = mod[5:6, :]
    mix = jnp.dot(mix_scr[...], wout_ref[...], preferred_element_type=F32)
    yield
    x = x_ref[0] + gate1 * mix
    h = ((x * _rms_scale(x)) * (g_ref[...] * (1.0 + scale2)) + shift2).astype(BF16)
    yield

    for c in range(D_FF // FF_CHUNK):
        cols = slice(c * FF_CHUNK, (c + 1) * FF_CHUNK)
        a = jnp.dot(h, w1_ref[:, cols], preferred_element_type=F32)
        a = jnp.maximum(a, 0.0)
        a = (a * a).astype(BF16)
        yield
        part = jnp.dot(a, w2_ref[cols, :], preferred_element_type=F32)
        if c == 0:
            acc_scr[...] = part
        else:
            acc_scr[...] += part
        yield
    y = x + gate2 * acc_scr[...]
    o_ref[0] = (y * _rms_scale(y)) * gf_ref[...]


def _block_kernel(sinks_ref, xa_ref, xb_ref, pos_ref, mod_ref, gm_ref, win_ref, wcat_ref, sb_ref,
                  invf_ref, wout_ref, gffn_ref, w1_ref, w2_ref, gf_ref, o_ref,
                  mix_scr, acc_scr, k_scr, v_scr, bd_scr, lhs_scr, sv_scr, sc_scr, p_scr, inv_scr,
                  *, n_tiles, tiles_per_seq):
    i = pl.program_id(0)
    tile_a = jnp.minimum(i, n_tiles - 1)
    tile_b = jnp.maximum(i - 1, 0)

    @pl.when(i == 0)
    def _():
        mix_scr[...] = jnp.zeros_like(mix_scr)
        k_scr[:, 0:Q_BLOCK, :] = jnp.zeros((N_KV_HEADS, Q_BLOCK, LANES), BF16)
        v_scr[:, 0:Q_BLOCK] = jnp.zeros((KV_WIDTH, Q_BLOCK), BF16)

    ffn = _ffn_stream(mod_ref[tile_b // tiles_per_seq], xb_ref, mix_scr, wout_ref, gffn_ref, w1_ref,
                      w2_ref, gf_ref, o_ref, acc_scr)
    s_a = tile_a % tiles_per_seq
    mixer = _mixer_stream(s_a, s_a == tiles_per_seq - 1, mod_ref[tile_a // tiles_per_seq], sinks_ref,
                          xa_ref, pos_ref, gm_ref, win_ref, wcat_ref, sb_ref, invf_ref, mix_scr,
                          k_scr, v_scr, bd_scr, lhs_scr, sv_scr, sc_scr, p_scr, inv_scr)
    order = ("ab"
             "ab"
             "baabaabaa"
             "a"
             "baabaabaaabaabaaa"
             "ab")
    streams = {"a": mixer, "b": ffn}
    for name in order:
        next(streams[name], None)
    for stream in (mixer, ffn):
        for _ in stream:
            pass


def _inv_freq_column():
    inv_freq = ROPE_THETA ** (-np.arange(0, ROT_DIM, 2, dtype=np.float32) / ROT_DIM)
    return inv_freq.astype(np.float32)[:, None]


def _const_spec(shape):
    zeros = (0,) * len(shape)
    return pl.BlockSpec(shape, lambda *_: zeros, pipeline_mode=pl.Buffered(1))


def kernel(x, c, positions, w_ada, b_ada, g_mix, w_in, w_spatial, b_spatial, sinks, w_out, g_ffn,
           w_ff1, w_ff2, g_final):
    B, S, D = x.shape
    ts = TOKENS_PER_STEP
    assert D == D_MODEL and S % ts == 0 and w_ada.shape[0] == 1

    n_ada = N_MOD * D
    mod = pl.pallas_call(
        _ada_kernel,
        grid=(n_ada // ADA_COLS,),
        in_specs=[pl.BlockSpec((B, D), lambda i: (0, 0)),
                  pl.BlockSpec((D, ADA_COLS), lambda i: (0, i)),
                  pl.BlockSpec((1, ADA_COLS), lambda i: (0, i))],
        out_specs=pl.BlockSpec((B, ADA_COLS), lambda i: (0, i)),
        out_shape=jax.ShapeDtypeStruct((B, n_ada), F32),
        compiler_params=pltpu.CompilerParams(dimension_semantics=("arbitrary",),
                                             vmem_limit_bytes=VMEM_LIMIT_BYTES),
        name="ada_mod",
    )(c, w_ada[0], b_ada[0][None])
    mod = mod.reshape(B, N_MOD, D)

    ws = w_spatial[0]
    wcat = ws.reshape(N_PAIRS, 2, CHUNK, CHUNK).transpose(0, 2, 1, 3).reshape(
        N_PAIRS, CHUNK, 2 * CHUNK)
    sbias = jnp.repeat(b_spatial[0].T, HEAD_DIM, axis=1)
    pos3 = positions.reshape(B, 1, S)

    tiles_per_seq = S // ts
    n_tiles = B * tiles_per_seq

    def tile_a(i):
        return jnp.minimum(i, n_tiles - 1)

    def tile_b(i):
        return jnp.maximum(i - 1, 0)

    def tok_spec(tile):
        return pl.BlockSpec((1, ts, D), lambda i: (tile(i) // tiles_per_seq, tile(i) % tiles_per_seq, 0))

    return pl.pallas_call(
        functools.partial(_block_kernel, n_tiles=n_tiles, tiles_per_seq=tiles_per_seq),
        grid=(n_tiles + 1,),
        in_specs=[pl.BlockSpec(memory_space=pltpu.SMEM),
                  tok_spec(tile_a),
                  tok_spec(tile_b),
                  pl.BlockSpec((1, 1, ts),
                               lambda i: (tile_a(i) // tiles_per_seq, 0, tile_a(i) % tiles_per_seq)),
                  _const_spec((B, N_MOD, D)),
                  _const_spec((1, D)),
                  _const_spec((D, IN_PROJ_WIDTH)),
                  _const_spec((N_PAIRS, CHUNK, 2 * CHUNK)),
                  _const_spec((CHUNK, GMLP_WIDTH)),
                  _const_spec((ROT_DIM // 2, 1)),
                  _const_spec((D, D)),
                  _const_spec((1, D)),
                  _const_spec((D, D_FF)),
                  _const_spec((D_FF, D)),
                  _const_spec((1, D))],
        out_specs=tok_spec(tile_b),
        out_shape=jax.ShapeDtypeStruct((B, S, D), F32),
        scratch_shapes=[pltpu.VMEM((ts, D), BF16),
                        pltpu.VMEM((ts, D), F32),
                        pltpu.VMEM((N_KV_HEADS, Q_BLOCK + ts, LANES), BF16),
                        pltpu.VMEM((KV_WIDTH, Q_BLOCK + ts), BF16),
                        pltpu.VMEM((N_SUB * N_PAIRS, 2 * CHUNK, LANES), BF16),
                        pltpu.VMEM((N_ATT, GQA_GROUP * Q_BLOCK, LANES), BF16),
                        pltpu.VMEM((ts, GMLP_WIDTH), F32),
                        pltpu.VMEM((N_ATT, 2 * Q_BLOCK, GQA_GROUP * Q_BLOCK), F32),
                        pltpu.VMEM((N_ATT, 2 * Q_BLOCK, GQA_GROUP * Q_BLOCK), BF16),
                        pltpu.VMEM((N_ATT, 1, GQA_GROUP * Q_BLOCK), F32)],
        compiler_params=pltpu.CompilerParams(dimension_semantics=("arbitrary",),
                                             vmem_limit_bytes=VMEM_LIMIT_BYTES),
        name="decoder_block",
    )(sinks[0], x, x, pos3, mod, g_mix, w_in[0].astype(BF16), wcat, sbias,
      jnp.asarray(_inv_freq_column()), w_out[0].astype(BF16), g_ffn,
      w_ff1[0].astype(BF16), w_ff2[0].astype(BF16), g_final[None])
```

```python
import functools
import math

import numpy as np
import jax
import jax.numpy as jnp
from jax import lax
from jax.experimental import pallas as pl
from jax.experimental.pallas import tpu as pltpu

D_MODEL = 1024
HEAD_DIM = 64
GMLP_GROUPS = 8
GMLP_WIDTH = GMLP_GROUPS * HEAD_DIM
CHUNK = 128
N_Q_HEADS = 8
N_KV_HEADS = 2
GQA_GROUP = N_Q_HEADS // N_KV_HEADS
ATTN_WIDTH = N_Q_HEADS * HEAD_DIM
KV_WIDTH = N_KV_HEADS * HEAD_DIM
WINDOW = 128
Q_BLOCK = 128
ROPE_THETA = 500000.0
ROT_DIM = HEAD_DIM // 4
D_FF = 4 * D_MODEL
IN_PROJ_WIDTH = 2 * GMLP_WIDTH + ATTN_WIDTH + 2 * KV_WIDTH
N_MOD = 6
EPS = 1e-5
LOG2E = math.log2(math.e)

LANES = 128
SUBLANES = 8
TOKENS_PER_STEP = 512
FF_CHUNK = 1024
ADA_COLS = 1536
VMEM_LIMIT_BYTES = 60 * 1024 * 1024

N_SUB = TOKENS_PER_STEP // Q_BLOCK
N_ATT = N_SUB * N_KV_HEADS
N_PAIRS = GMLP_GROUPS // 2

F32 = jnp.float32
BF16 = jnp.bfloat16


def _ada_kernel(c_ref, w_ref, b_ref, o_ref):
    c = c_ref[...]
    c_act = c * (1.0 / (1.0 + jnp.exp(-c)))
    o_ref[...] = jnp.dot(c_act, w_ref[...], preferred_element_type=F32) + b_ref[...]


def _rms_scale(x):
    return lax.rsqrt(jnp.mean(x * x, axis=-1, keepdims=True) + EPS)


def _gelu_tanh(x):
    c = 2.0 * math.sqrt(2.0 / math.pi) * LOG2E
    return x / (1.0 + jnp.exp2(x * (-c - (c * 0.044715) * (x * x))))


def _mixer_stream(s, is_last_in_seq, mod, sinks_ref, x_ref, pos_ref, g_ref, win_ref, wcat_ref, sb_ref,
                  invf_ref, mix_scr, k_scr, v_scr, bd_scr, lhs_scr, sv_scr, sc_scr, p_scr, inv_scr):
    ts = TOKENS_PER_STEP
    x = x_ref[0]
    shift1 = mod[0:1, :]
    scale1 = mod[1:2, :]

    h = ((x * _rms_scale(x)) * (g_ref[...] * (1.0 + scale1)) + shift1).astype(BF16)
    yield
    proj = jnp.dot(h, win_ref[...], preferred_element_type=F32)
    yield

    lane = lax.broadcasted_iota(jnp.int32, (1, LANES), 1)
    left = lane < HEAD_DIM

    t_idx = lax.broadcasted_iota(jnp.int32, (CHUNK, 2 * CHUNK), 0)
    s_idx = lax.broadcasted_iota(jnp.int32, (CHUNK, 2 * CHUNK), 1)
    causal = (s_idx & (CHUNK - 1)) <= t_idx
    w_pairs = [jnp.where(causal, wcat_ref[p], 0.0).astype(BF16) for p in range(N_PAIRS)]
    for c in range(ts // CHUNK):
        rows = slice(c * CHUNK, (c + 1) * CHUNK)
        v = _gelu_tanh(proj[rows, GMLP_WIDTH:2 * GMLP_WIDTH])
        for p in range(N_PAIRS):
            vp = v[:, p * LANES:(p + 1) * LANES]
            bd_scr[c * N_PAIRS + p, 0:CHUNK, :] = jnp.where(left, vp, 0.0).astype(BF16)
            bd_scr[c * N_PAIRS + p, CHUNK:2 * CHUNK, :] = jnp.where(left, 0.0, vp).astype(BF16)
        yield

    ang = invf_ref[...] * pos_ref[0].astype(F32)
    cos8 = jnp.cos(ang)
    sin8 = jnp.sin(ang)
    one8 = jnp.ones_like(ang)
    zero8 = jnp.zeros_like(ang)
    tiles_per_head = HEAD_DIM // SUBLANES
    heads_per_vreg = LANES // HEAD_DIM
    cos_t = jnp.concatenate(([cos8, cos8] + [one8] * (tiles_per_head - 2)) * heads_per_vreg, axis=0).T
    sin_a = jnp.concatenate(([-sin8] + [zero8] * (tiles_per_head - 1)) * heads_per_vreg, axis=0).T
    sin_b = jnp.concatenate(([zero8, sin8] + [zero8] * (tiles_per_head - 2)) * heads_per_vreg, axis=0).T

    def rope(t):
        return (t * cos_t + pltpu.roll(t, LANES - ROT_DIM // 2, 1) * sin_a
                + pltpu.roll(t, ROT_DIM // 2, 1) * sin_b)

    o = 2 * GMLP_WIDTH
    scale = LOG2E / math.sqrt(HEAD_DIM)
    zero = jnp.zeros((), BF16)
    for i in range(ATTN_WIDTH // LANES):
        qp = (rope(proj[:, o + i * LANES:o + (i + 1) * LANES]) * scale).astype(BF16)
        hk, m = divmod(i, 2)
        halves = (jnp.where(left, qp, zero), jnp.where(left, zero, qp))
        for j in range(N_SUB):
            for e in range(2):
                g = 2 * m + e
                lhs_scr[j * N_KV_HEADS + hk, g * Q_BLOCK:(g + 1) * Q_BLOCK, :] = (
                    halves[e][j * Q_BLOCK:(j + 1) * Q_BLOCK])
    yield
    o += ATTN_WIDTH
    k = rope(proj[:, o:o + KV_WIDTH])
    v_att = proj[:, o + KV_WIDTH:o + 2 * KV_WIDTH]
    k_sw = pltpu.roll(k, HEAD_DIM, 1)
    k_scr[0, Q_BLOCK:Q_BLOCK + ts, :] = jnp.where(left, k, k_sw).astype(BF16)
    k_scr[1, Q_BLOCK:Q_BLOCK + ts, :] = jnp.where(left, k_sw, k).astype(BF16)
    v_scr[:, Q_BLOCK:Q_BLOCK + ts] = v_att.T.astype(BF16)
    yield

    for c in range(ts // CHUNK):
        rows = slice(c * CHUNK, (c + 1) * CHUNK)
        for p in range(N_PAIRS):
            sv_scr[rows, p * LANES:(p + 1) * LANES] = jnp.dot(
                w_pairs[p], bd_scr[c * N_PAIRS + p], preferred_element_type=F32)

    def scores(n):
        j, hk = divmod(n, N_KV_HEADS)
        k_win = k_scr[hk, j * Q_BLOCK:(j + 2) * Q_BLOCK, :]
        sc_scr[n] = lax.dot_general(k_win, lhs_scr[n], (((1,), (1,)), ((), ())),
                                    preferred_element_type=F32)

    kj = lax.broadcasted_iota(jnp.int32, (2 * Q_BLOCK, Q_BLOCK), 0)
    qi = lax.broadcasted_iota(jnp.int32, (2 * Q_BLOCK, Q_BLOCK), 1)
    band = (kj > qi) & (kj <= qi + WINDOW)
    bias_band = jnp.where(band, 0.0, -jnp.inf).astype(F32)
    bias_first = jnp.where(band & (kj >= Q_BLOCK), 0.0, -jnp.inf).astype(F32)

    def softmax(n):
        j, hk = divmod(n, N_KV_HEADS)
        bias = jnp.where(s == 0, bias_first, bias_band) if j == 0 else bias_band
        sink = jnp.concatenate(
            [jnp.full((1, Q_BLOCK), sinks_ref[hk * GQA_GROUP + g] * LOG2E, F32)
             for g in range(GQA_GROUP)], axis=1)
        sc = jnp.concatenate(
            [sc_scr[n, :, g * Q_BLOCK:(g + 1) * Q_BLOCK] + bias for g in range(GQA_GROUP)], axis=1)
        mx = jnp.maximum(jnp.max(sc, axis=0, keepdims=True), sink)
        p = jnp.exp2(sc - mx)
        denom = jnp.sum(p, axis=0, keepdims=True) + jnp.exp2(sink - mx)
        p_scr[n] = p.astype(BF16)
        inv_scr[n] = 1.0 / denom

    def values(n):
        j, hk = divmod(n, N_KV_HEADS)
        rows = slice(j * Q_BLOCK, (j + 1) * Q_BLOCK)
        v_win = v_scr[hk * HEAD_DIM:(hk + 1) * HEAD_DIM, j * Q_BLOCK:(j + 2) * Q_BLOCK]
        pv = jnp.dot(v_win, p_scr[n], preferred_element_type=F32) * inv_scr[n]
        for m in range(2):
            pair = jnp.concatenate([pv[:, (2 * m) * Q_BLOCK:(2 * m + 1) * Q_BLOCK],
                                    pv[:, (2 * m + 1) * Q_BLOCK:(2 * m + 2) * Q_BLOCK]], axis=0)
            col = GMLP_WIDTH + (2 * hk + m) * LANES
            mix_scr[rows, col:col + LANES] = pair.T.astype(BF16)

    def gate(c):
        rows = slice(c * CHUNK, (c + 1) * CHUNK)
        u = _gelu_tanh(proj[rows, 0:GMLP_WIDTH])
        mix_scr[rows, 0:GMLP_WIDTH] = (u * (sv_scr[rows, :] + sb_ref[...])).astype(BF16)

    scores(0); softmax(0); scores(1); softmax(1)
    for c in range(ts // CHUNK):
        gate(c)
    yield
    scores(2); softmax(2); scores(3); softmax(3)
    yield
    values(0); values(1); scores(4); softmax(4); scores(5); softmax(5)
    yield
    values(2); values(3); scores(6); softmax(6); scores(7); softmax(7)
    yield
    values(4); values(5)
    yield
    values(6); values(7)

    k_last = k_scr[:, ts:ts + Q_BLOCK, :]
    v_last = v_scr[:, ts:ts + Q_BLOCK]
    k_scr[:, 0:Q_BLOCK, :] = jnp.where(is_last_in_seq, jnp.zeros_like(k_last), k_last)
    v_scr[:, 0:Q_BLOCK] = jnp.where(is_last_in_seq, jnp.zeros_like(v_last), v_last)


def _ffn_stream(mod, x_ref, mix_scr, wout_ref, g_ref, w1_ref, w2_ref, gf_ref, o_ref, acc_scr):
    gate1 = mod[2:3, :]
    shift2 = mod[3:4, :]
    scale2 = mod[4:5, :]
    gate2 = mod[5:6, :]
    mix = jnp.dot(mix_scr[...], wout_ref[...], preferred_element_type=F32)
    yield
    x = x_ref[0] + gate1 * mix
    h = ((x * _rms_scale(x)) * (g_ref[...] * (1.0 + scale2)) + shift2).astype(BF16)
    yield

    for c in range(D_FF // FF_CHUNK):
        cols = slice(c * FF_CHUNK, (c + 1) * FF_CHUNK)
        a = jnp.dot(h, w1_ref[:, cols], preferred_element_type=F32)
        a = jnp.maximum(a, 0.0)
        a = (a * a).astype(BF16)
        yield
        part = jnp.dot(a, w2_ref[cols, :], preferred_element_type=F32)
        if c == 0:
            acc_scr[...] = part
        else:
            acc_scr[...] += part
        yield
    y = x + gate2 * acc_scr[...]
    o_ref[0] = (y * _rms_scale(y)) * gf_ref[...]


def _block_kernel(sinks_ref, xa_ref, xb_ref, pos_ref, mod_ref, gm_ref, win_ref, wcat_ref, sb_ref,
                  invf_ref, wout_ref, gffn_ref, w1_ref, w2_ref, gf_ref, o_ref,
                  mix_scr, acc_scr, k_scr, v_scr, bd_scr, lhs_scr, sv_scr, sc_scr, p_scr, inv_scr,
                  *, n_tiles, tiles_per_seq):
    i = pl.program_id(0)
    tile_a = jnp.minimum(i, n_tiles - 1)
    tile_b = jnp.maximum(i - 1, 0)

    @pl.when(i == 0)
    def _():
        mix_scr[...] = jnp.zeros_like(mix_scr)
        k_scr[:, 0:Q_BLOCK, :] = jnp.zeros((N_KV_HEADS, Q_BLOCK, LANES), BF16)
        v_scr[:, 0:Q_BLOCK] = jnp.zeros((KV_WIDTH, Q_BLOCK), BF16)

    ffn = _ffn_stream(mod_ref[tile_b // tiles_per_seq], xb_ref, mix_scr, wout_ref, gffn_ref, w1_ref,
                      w2_ref, gf_ref, o_ref, acc_scr)
    s_a = tile_a % tiles_per_seq
    mixer = _mixer_stream(s_a, s_a == tiles_per_seq - 1, mod_ref[tile_a // tiles_per_seq], sinks_ref,
                          xa_ref, pos_ref, gm_ref, win_ref, wcat_ref, sb_ref, invf_ref, mix_scr,
                          k_scr, v_scr, bd_scr, lhs_scr, sv_scr, sc_scr, p_scr, inv_scr)
    order = ("ab"
             "ab"
             "baabaabaa"
             "ababababab"
             "ab")
    streams = {"a": mixer, "b": ffn}
    for name in order:
        next(streams[name], None)
    for stream in (mixer, ffn):
        for _ in stream:
            pass


def _inv_freq_column():
    inv_freq = ROPE_THETA ** (-np.arange(0, ROT_DIM, 2, dtype=np.float32) / ROT_DIM)
    return inv_freq.astype(np.float32)[:, None]


def _const_spec(shape):
    zeros = (0,) * len(shape)
    return pl.BlockSpec(shape, lambda *_: zeros, pipeline_mode=pl.Buffered(1))


def kernel(x, c, positions, w_ada, b_ada, g_mix, w_in, w_spatial, b_spatial, sinks, w_out, g_ffn,
           w_ff1, w_ff2, g_final):
    B, S, D = x.shape
    ts = TOKENS_PER_STEP
    assert D == D_MODEL and S % ts == 0 and w_ada.shape[0] == 1

    n_ada = N_MOD * D
    mod = pl.pallas_call(
        _ada_kernel,
        grid=(n_ada // ADA_COLS,),
        in_specs=[pl.BlockSpec((B, D), lambda i: (0, 0)),
                  pl.BlockSpec((D, ADA_COLS), lambda i: (0, i)),
                  pl.BlockSpec((1, ADA_COLS), lambda i: (0, i))],
        out_specs=pl.BlockSpec((B, ADA_COLS), lambda i: (0, i)),
        out_shape=jax.ShapeDtypeStruct((B, n_ada), F32),
        compiler_params=pltpu.CompilerParams(dimension_semantics=("arbitrary",),
                                             vmem_limit_bytes=VMEM_LIMIT_BYTES),
        name="ada_mod",
    )(c, w_ada[0], b_ada[0][None])
    mod = mod.reshape(B, N_MOD, D)

    ws = w_spatial[0]
    wcat = ws.reshape(N_PAIRS, 2, CHUNK, CHUNK).transpose(0, 2, 1, 3).reshape(
        N_PAIRS, CHUNK, 2 * CHUNK)
    sbias = jnp.repeat(b_spatial[0].T, HEAD_DIM, axis=1)
    pos3 = positions.reshape(B, 1, S)

    tiles_per_seq = S // ts
    n_tiles = B * tiles_per_seq

    def tile_a(i):
        return jnp.minimum(i, n_tiles - 1)

    def tile_b(i):
        return jnp.maximum(i - 1, 0)

    def tok_spec(tile):
        return pl.BlockSpec((1, ts, D), lambda i: (tile(i) // tiles_per_seq, tile(i) % tiles_per_seq, 0))

    return pl.pallas_call(
        functools.partial(_block_kernel, n_tiles=n_tiles, tiles_per_seq=tiles_per_seq),
        grid=(n_tiles + 1,),
        in_specs=[pl.BlockSpec(memory_space=pltpu.SMEM),
                  tok_spec(tile_a),
                  tok_spec(tile_b),
                  pl.BlockSpec((1, 1, ts),
                               lambda i: (tile_a(i) // tiles_per_seq, 0, tile_a(i) % tiles_per_seq)),
                  _const_spec((B, N_MOD, D)),
                  _const_spec((1, D)),
                  _const_spec((D, IN_PROJ_WIDTH)),
                  _const_spec((N_PAIRS, CHUNK, 2 * CHUNK)),
                  _const_spec((CHUNK, GMLP_WIDTH)),
                  _const_spec((ROT_DIM // 2, 1)),
                  _const_spec((D, D)),
                  _const_spec((1, D)),
                  _const_spec((D, D_FF)),
                  _const_spec((D_FF, D)),
                  _const_spec((1, D))],
        out_specs=tok_spec(tile_b),
        out_shape=jax.ShapeDtypeStruct((B, S, D), F32),
        scratch_shapes=[pltpu.VMEM((ts, D), BF16),
                        pltpu.VMEM((ts, D), F32),
                        pltpu.VMEM((N_KV_HEADS, Q_BLOCK + ts, LANES), BF16),
                        pltpu.VMEM((KV_WIDTH, Q_BLOCK + ts), BF16),
                        pltpu.VMEM((N_SUB * N_PAIRS, 2 * CHUNK, LANES), BF16),
                        pltpu.VMEM((N_ATT, GQA_GROUP * Q_BLOCK, LANES), BF16),
                        pltpu.VMEM((ts, GMLP_WIDTH), F32),
                        pltpu.VMEM((N_ATT, 2 * Q_BLOCK, GQA_GROUP * Q_BLOCK), F32),
                        pltpu.VMEM((N_ATT, 2 * Q_BLOCK, GQA_GROUP * Q_BLOCK), BF16),
                        pltpu.VMEM((N_ATT, 1, GQA_GROUP * Q_BLOCK), F32)],
        compiler_params=pltpu.CompilerParams(dimension_semantics=("arbitrary",),
                                             vmem_limit_bytes=VMEM_LIMIT_BYTES),
        name="decoder_block",
    )(sinks[0], x, x, pos3, mod, g_mix, w_in[0].astype(BF16), wcat, sbias,
      jnp.asarray(_inv_freq_column()), w_out[0].astype(BF16), g_ffn,
      w_ff1[0].astype(BF16), w_ff2[0].astype(BF16), g_final[None])
```

```python
import functools
import math

import numpy as np
import jax
import jax.numpy as jnp
from jax import lax
from jax.experimental import pallas as pl
from jax.experimental.pallas import tpu as pltpu

D_MODEL = 1024
HEAD_DIM = 64
GMLP_GROUPS = 8
GMLP_WIDTH = GMLP_GROUPS * HEAD_DIM
CHUNK = 128
N_Q_HEADS = 8
N_KV_HEADS = 2
GQA_GROUP = N_Q_HEADS // N_KV_HEADS
ATTN_WIDTH = N_Q_HEADS * HEAD_DIM
KV_WIDTH = N_KV_HEADS * HEAD_DIM
WINDOW = 128
Q_BLOCK = 128
ROPE_THETA = 500000.0
ROT_DIM = HEAD_DIM // 4
D_FF = 4 * D_MODEL
IN_PROJ_WIDTH = 2 * GMLP_WIDTH + ATTN_WIDTH + 2 * KV_WIDTH
N_MOD = 6
EPS = 1e-5
LOG2E = math.log2(math.e)

LANES = 128
SUBLANES = 8
TOKENS_PER_STEP = 512
FF_CHUNK = 1024
FF2_COL_BLOCKS = 4
ADA_COLS = 1536
VMEM_LIMIT_BYTES = 60 * 1024 * 1024

N_SUB = TOKENS_PER_STEP // Q_BLOCK
N_ATT = N_SUB * N_KV_HEADS
N_PAIRS = GMLP_GROUPS // 2

F32 = jnp.float32
BF16 = jnp.bfloat16


def _ada_kernel(c_ref, w_ref, b_ref, o_ref):
    c = c_ref[...]
    c_act = c * (1.0 / (1.0 + jnp.exp(-c)))
    o_ref[...] = jnp.dot(c_act, w_ref[...], preferred_element_type=F32) + b_ref[...]


def _rms_scale(x):
    return lax.rsqrt(jnp.mean(x * x, axis=-1, keepdims=True) + EPS)


def _gelu_tanh(x):
    c = 2.0 * math.sqrt(2.0 / math.pi) * LOG2E
    return x / (1.0 + jnp.exp2(x * (-c - (c * 0.044715) * (x * x))))


def _mixer_stream(s, is_last_in_seq, mod, sinks_ref, x_ref, pos_ref, g_ref, win_ref, wcat_ref, sb_ref,
                  invf_ref, mix_scr, k_scr, v_scr, bd_scr, lhs_scr, sv_scr, sc_scr, p_scr, inv_scr):
    ts = TOKENS_PER_STEP
    x = x_ref[0]
    shift1 = mod[0:1, :]
    scale1 = mod[1:2, :]

    h = ((x * _rms_scale(x)) * (g_ref[...] * (1.0 + scale1)) + shift1).astype(BF16)
    yield
    proj = jnp.dot(h, win_ref[...], preferred_element_type=F32)
    yield

    lane = lax.broadcasted_iota(jnp.int32, (1, LANES), 1)
    left = lane < HEAD_DIM

    t_idx = lax.broadcasted_iota(jnp.int32, (CHUNK, 2 * CHUNK), 0)
    s_idx = lax.broadcasted_iota(jnp.int32, (CHUNK, 2 * CHUNK), 1)
    causal = (s_idx & (CHUNK - 1)) <= t_idx
    w_pairs = [jnp.where(causal, wcat_ref[p], 0.0).astype(BF16) for p in range(N_PAIRS)]
    for c in range(ts // CHUNK):
        rows = slice(c * CHUNK, (c + 1) * CHUNK)
        v = _gelu_tanh(proj[rows, GMLP_WIDTH:2 * GMLP_WIDTH])
        for p in range(N_PAIRS):
            vp = v[:, p * LANES:(p + 1) * LANES]
            bd_scr[c * N_PAIRS + p, 0:CHUNK, :] = jnp.where(left, vp, 0.0).astype(BF16)
            bd_scr[c * N_PAIRS + p, CHUNK:2 * CHUNK, :] = jnp.where(left, 0.0, vp).astype(BF16)
        yield

    ang = invf_ref[...] * pos_ref[0].astype(F32)
    cos8 = jnp.cos(ang)
    sin8 = jnp.sin(ang)
    one8 = jnp.ones_like(ang)
    zero8 = jnp.zeros_like(ang)
    tiles_per_head = HEAD_DIM // SUBLANES
    heads_per_vreg = LANES // HEAD_DIM
    cos_t = jnp.concatenate(([cos8, cos8] + [one8] * (tiles_per_head - 2)) * heads_per_vreg, axis=0).T
    sin_a = jnp.concatenate(([-sin8] + [zero8] * (tiles_per_head - 1)) * heads_per_vreg, axis=0).T
    sin_b = jnp.concatenate(([zero8, sin8] + [zero8] * (tiles_per_head - 2)) * heads_per_vreg, axis=0).T

    def rope(t):
        return (t * cos_t + pltpu.roll(t, LANES - ROT_DIM // 2, 1) * sin_a
                + pltpu.roll(t, ROT_DIM // 2, 1) * sin_b)

    o = 2 * GMLP_WIDTH
    scale = LOG2E / math.sqrt(HEAD_DIM)
    zero = jnp.zeros((), BF16)
    for i in range(ATTN_WIDTH // LANES):
        qp = (rope(proj[:, o + i * LANES:o + (i + 1) * LANES]) * scale).astype(BF16)
        hk, m = divmod(i, 2)
        halves = (jnp.where(left, qp, zero), jnp.where(left, zero, qp))
        for j in range(N_SUB):
            for e in range(2):
                g = 2 * m + e
                lhs_scr[j * N_KV_HEADS + hk, g * Q_BLOCK:(g + 1) * Q_BLOCK, :] = (
                    halves[e][j * Q_BLOCK:(j + 1) * Q_BLOCK])
    yield
    o += ATTN_WIDTH
    k = rope(proj[:, o:o + KV_WIDTH])
    v_att = proj[:, o + KV_WIDTH:o + 2 * KV_WIDTH]
    k_sw = pltpu.roll(k, HEAD_DIM, 1)
    k_scr[0, Q_BLOCK:Q_BLOCK + ts, :] = jnp.where(left, k, k_sw).astype(BF16)
    k_scr[1, Q_BLOCK:Q_BLOCK + ts, :] = jnp.where(left, k_sw, k).astype(BF16)
    v_scr[:, Q_BLOCK:Q_BLOCK + ts] = v_att.T.astype(BF16)
    yield

    for c in range(ts // CHUNK):
        rows = slice(c * CHUNK, (c + 1) * CHUNK)
        for p in range(N_PAIRS):
            sv_scr[rows, p * LANES:(p + 1) * LANES] = jnp.dot(
                w_pairs[p], bd_scr[c * N_PAIRS + p], preferred_element_type=F32)

    def scores(n):
        j, hk = divmod(n, N_KV_HEADS)
        k_win = k_scr[hk, j * Q_BLOCK:(j + 2) * Q_BLOCK, :]
        sc_scr[n] = lax.dot_general(k_win, lhs_scr[n], (((1,), (1,)), ((), ())),
                                    preferred_element_type=F32)

    kj = lax.broadcasted_iota(jnp.int32, (2 * Q_BLOCK, Q_BLOCK), 0)
    qi = lax.broadcasted_iota(jnp.int32, (2 * Q_BLOCK, Q_BLOCK), 1)
    band = (kj > qi) & (kj <= qi + WINDOW)
    bias_band = jnp.where(band, 0.0, -jnp.inf).astype(F32)
    bias_first = jnp.where(band & (kj >= Q_BLOCK), 0.0, -jnp.inf).astype(F32)

    def softmax(n):
        j, hk = divmod(n, N_KV_HEADS)
        bias = jnp.where(s == 0, bias_first, bias_band) if j == 0 else bias_band
        sink = jnp.concatenate(
            [jnp.full((1, Q_BLOCK), sinks_ref[hk * GQA_GROUP + g] * LOG2E, F32)
             for g in range(GQA_GROUP)], axis=1)
        sc = jnp.concatenate(
            [sc_scr[n, :, g * Q_BLOCK:(g + 1) * Q_BLOCK] + bias for g in range(GQA_GROUP)], axis=1)
        mx = jnp.maximum(jnp.max(sc, axis=0, keepdims=True), sink)
        p = jnp.exp2(sc - mx)
        denom = jnp.sum(p, axis=0, keepdims=True) + jnp.exp2(sink - mx)
        p_scr[n] = p.astype(BF16)
        inv_scr[n] = 1.0 / denom

    def values(n):
        j, hk = divmod(n, N_KV_HEADS)
        rows = slice(j * Q_BLOCK, (j + 1) * Q_BLOCK)
        v_win = v_scr[hk * HEAD_DIM:(hk + 1) * HEAD_DIM, j * Q_BLOCK:(j + 2) * Q_BLOCK]
        pv = jnp.dot(v_win, p_scr[n], preferred_element_type=F32) * inv_scr[n]
        for m in range(2):
            pair = jnp.concatenate([pv[:, (2 * m) * Q_BLOCK:(2 * m + 1) * Q_BLOCK],
                                    pv[:, (2 * m + 1) * Q_BLOCK:(2 * m + 2) * Q_BLOCK]], axis=0)
            col = GMLP_WIDTH + (2 * hk + m) * LANES
            mix_scr[rows, col:col + LANES] = pair.T.astype(BF16)

    def gate(c):
        rows = slice(c * CHUNK, (c + 1) * CHUNK)
        u = _gelu_tanh(proj[rows, 0:GMLP_WIDTH])
        mix_scr[rows, 0:GMLP_WIDTH] = (u * (sv_scr[rows, :] + sb_ref[...])).astype(BF16)

    scores(0); softmax(0); scores(1); softmax(1)
    for c in range(ts // CHUNK):
        gate(c)
    yield
    scores(2); softmax(2); scores(3); softmax(3)
    yield
    values(0); values(1); scores(4); softmax(4); scores(5); softmax(5)
    yield
    values(2); values(3); scores(6); softmax(6); scores(7); softmax(7)
    yield
    values(4); values(5)
    yield
    values(6); values(7)

    k_last = k_scr[:, ts:ts + Q_BLOCK, :]
    v_last = v_scr[:, ts:ts + Q_BLOCK]
    k_scr[:, 0:Q_BLOCK, :] = jnp.where(is_last_in_seq, jnp.zeros_like(k_last), k_last)
    v_scr[:, 0:Q_BLOCK] = jnp.where(is_last_in_seq, jnp.zeros_like(v_last), v_last)


def _ffn_stream(mod, x_ref, mix_scr, wout_ref, g_ref, w1_ref, w2_ref, gf_ref, o_ref, hid_scr, y_scr):
    gate1 = mod[2:3, :]
    shift2 = mod[3:4, :]
    scale2 = mod[4:5, :]
    gate2 = mod[5:6, :]
    mix = jnp.dot(mix_scr[...], wout_ref[...], preferred_element_type=F32)
    yield
    x = x_ref[0] + gate1 * mix
    h = ((x * _rms_scale(x)) * (g_ref[...] * (1.0 + scale2)) + shift2).astype(BF16)
    yield

    for c in range(D_FF // FF_CHUNK):
        cols = slice(c * FF_CHUNK, (c + 1) * FF_CHUNK)
        a = jnp.dot(h, w1_ref[:, cols], preferred_element_type=F32)
        a = jnp.maximum(a, 0.0)
        hid_scr[:, cols] = (a * a).astype(BF16)
        yield
    wcol = D_MODEL // FF2_COL_BLOCKS
    gf = gf_ref[...]

    def finish(rows, cols, ff):
        y = x[rows, cols] + gate2[:, cols] * ff
        y_scr[rows, cols] = y * gf[:, cols]
        sq = y * y
        return sum(sq[:, l * LANES:(l + 1) * LANES] for l in range(wcol // LANES))

    ssq = None
    for q in range(FF2_COL_BLOCKS - 1):
        cols = slice(q * wcol, (q + 1) * wcol)
        ff = jnp.dot(hid_scr[...], w2_ref[:, cols], preferred_element_type=F32)
        part = finish(slice(None), cols, ff)
        ssq = part if ssq is None else ssq + part
        yield
    cols = slice((FF2_COL_BLOCKS - 1) * wcol, FF2_COL_BLOCKS * wcol)
    half = TOKENS_PER_STEP // 2
    for r in range(2):
        rows = slice(r * half, (r + 1) * half)
        ff = jnp.dot(hid_scr[rows, :], w2_ref[:, cols], preferred_element_type=F32)
        tot = ssq[rows] + finish(rows, cols, ff)
        scale = lax.rsqrt(jnp.sum(tot, axis=-1, keepdims=True) * (1.0 / D_MODEL) + EPS)
        o_ref[0, rows, :] = y_scr[rows, :] * scale
        yield


def _block_kernel(sinks_ref, xa_ref, xb_ref, pos_ref, mod_ref, gm_ref, win_ref, wcat_ref, sb_ref,
                  invf_ref, wout_ref, gffn_ref, w1_ref, w2_ref, gf_ref, o_ref,
                  mix_scr, hid_scr, k_scr, v_scr, bd_scr, lhs_scr, sv_scr, sc_scr, p_scr, inv_scr, y_scr,
                  *, n_tiles, tiles_per_seq):
    i = pl.program_id(0)
    tile_a = jnp.minimum(i, n_tiles - 1)
    tile_b = jnp.maximum(i - 1, 0)

    @pl.when(i == 0)
    def _():
        mix_scr[...] = jnp.zeros_like(mix_scr)
        k_scr[:, 0:Q_BLOCK, :] = jnp.zeros((N_KV_HEADS, Q_BLOCK, LANES), BF16)
        v_scr[:, 0:Q_BLOCK] = jnp.zeros((KV_WIDTH, Q_BLOCK), BF16)

    ffn = _ffn_stream(mod_ref[tile_b // tiles_per_seq], xb_ref, mix_scr, wout_ref, gffn_ref, w1_ref,
                      w2_ref, gf_ref, o_ref, hid_scr, y_scr)
    s_a = tile_a % tiles_per_seq
    mixer = _mixer_stream(s_a, s_a == tiles_per_seq - 1, mod_ref[tile_a // tiles_per_seq], sinks_ref,
                          xa_ref, pos_ref, gm_ref, win_ref, wcat_ref, sb_ref, invf_ref, mix_scr,
                          k_scr, v_scr, bd_scr, lhs_scr, sv_scr, sc_scr, p_scr, inv_scr)
    order = ("ab"
             "ab"
             "baabaabaa"
             "ababababab"
             "ab")
    streams = {"a": mixer, "b": ffn}
    for name in order:
        next(streams[name], None)
    for stream in (mixer, ffn):
        for _ in stream:
            pass


def _inv_freq_column():
    inv_freq = ROPE_THETA ** (-np.arange(0, ROT_DIM, 2, dtype=np.float32) / ROT_DIM)
    return inv_freq.astype(np.float32)[:, None]


def _const_spec(shape):
    zeros = (0,) * len(shape)
    return pl.BlockSpec(shape, lambda *_: zeros, pipeline_mode=pl.Buffered(1))


def kernel(x, c, positions, w_ada, b_ada, g_mix, w_in, w_spatial, b_spatial, sinks, w_out, g_ffn,
           w_ff1, w_ff2, g_final):
    B, S, D = x.shape
    ts = TOKENS_PER_STEP
    assert D == D_MODEL and S % ts == 0 and w_ada.shape[0] == 1

    n_ada = N_MOD * D
    mod = pl.pallas_call(
        _ada_kernel,
        grid=(n_ada // ADA_COLS,),
        in_specs=[pl.BlockSpec((B, D), lambda i: (0, 0)),
                  pl.BlockSpec((D, ADA_COLS), lambda i: (0, i)),
                  pl.BlockSpec((1, ADA_COLS), lambda i: (0, i))],
        out_specs=pl.BlockSpec((B, ADA_COLS), lambda i: (0, i)),
        out_shape=jax.ShapeDtypeStruct((B, n_ada), F32),
        compiler_params=pltpu.CompilerParams(dimension_semantics=("arbitrary",),
                                             vmem_limit_bytes=VMEM_LIMIT_BYTES),
        name="ada_mod",
    )(c, w_ada[0], b_ada[0][None])
    mod = mod.reshape(B, N_MOD, D)

    ws = w_spatial[0]
    wcat = ws.reshape(N_PAIRS, 2, CHUNK, CHUNK).transpose(0, 2, 1, 3).reshape(
        N_PAIRS, CHUNK, 2 * CHUNK)
    sbias = jnp.repeat(b_spatial[0].T, HEAD_DIM, axis=1)
    pos3 = positions.reshape(B, 1, S)

    tiles_per_seq = S // ts
    n_tiles = B * tiles_per_seq

    def tile_a(i):
        return jnp.minimum(i, n_tiles - 1)

    def tile_b(i):
        return jnp.maximum(i - 1, 0)

    def tok_spec(tile):
        return pl.BlockSpec((1, ts, D), lambda i: (tile(i) // tiles_per_seq, tile(i) % tiles_per_seq, 0))

    return pl.pallas_call(
        functools.partial(_block_kernel, n_tiles=n_tiles, tiles_per_seq=tiles_per_seq),
        grid=(n_tiles + 1,),
        in_specs=[pl.BlockSpec(memory_space=pltpu.SMEM),
                  tok_spec(tile_a),
                  tok_spec(tile_b),
                  pl.BlockSpec((1, 1, ts),
                               lambda i: (tile_a(i) // tiles_per_seq, 0, tile_a(i) % tiles_per_seq)),
                  _const_spec((B, N_MOD, D)),
                  _const_spec((1, D)),
                  _const_spec((D, IN_PROJ_WIDTH)),
                  _const_spec((N_PAIRS, CHUNK, 2 * CHUNK)),
                  _const_spec((CHUNK, GMLP_WIDTH)),
                  _const_spec((ROT_DIM // 2, 1)),
                  _const_spec((D, D)),
                  _const_spec((1, D)),
                  _const_spec((D, D_FF)),
                  _const_spec((D_FF, D)),
                  _const_spec((1, D))],
        out_specs=tok_spec(tile_b),
        out_shape=jax.ShapeDtypeStruct((B, S, D), F32),
        scratch_shapes=[pltpu.VMEM((ts, D), BF16),
                        pltpu.VMEM((ts, D_FF), BF16),
                        pltpu.VMEM((N_KV_HEADS, Q_BLOCK + ts, LANES), BF16),
                        pltpu.VMEM((KV_WIDTH, Q_BLOCK + ts), BF16),
                        pltpu.VMEM((N_SUB * N_PAIRS, 2 * CHUNK, LANES), BF16),
                        pltpu.VMEM((N_ATT, GQA_GROUP * Q_BLOCK, LANES), BF16),
                        pltpu.VMEM((ts, GMLP_WIDTH), F32),
                        pltpu.VMEM((N_ATT, 2 * Q_BLOCK, GQA_GROUP * Q_BLOCK), F32),
                        pltpu.VMEM((N_ATT, 2 * Q_BLOCK, GQA_GROUP * Q_BLOCK), BF16),
                        pltpu.VMEM((N_ATT, 1, GQA_GROUP * Q_BLOCK), F32),
                        pltpu.VMEM((ts, D), F32)],
        compiler_params=pltpu.CompilerParams(dimension_semantics=("arbitrary",),
                                             vmem_limit_bytes=VMEM_LIMIT_BYTES),
        name="decoder_block",
    )(sinks[0], x, x, pos3, mod, g_mix, w_in[0].astype(BF16), wcat, sbias,
      jnp.asarray(_inv_freq_column()), w_out[0].astype(BF16), g_ffn,
      w_ff1[0].astype(BF16), w_ff2[0].astype(BF16), g_final[None])
```

```python
import functools
import math

import numpy as np
import jax
import jax.numpy as jnp
from jax import lax
from jax.experimental import pallas as pl
from jax.experimental.pallas import tpu as pltpu

D_MODEL = 1024
HEAD_DIM = 64
GMLP_GROUPS = 8
GMLP_WIDTH = GMLP_GROUPS * HEAD_DIM
CHUNK = 128
N_Q_HEADS = 8
N_KV_HEADS = 2
GQA_GROUP = N_Q_HEADS // N_KV_HEADS
ATTN_WIDTH = N_Q_HEADS * HEAD_DIM
KV_WIDTH = N_KV_HEADS * HEAD_DIM
WINDOW = 128
Q_BLOCK = 128
ROPE_THETA = 500000.0
ROT_DIM = HEAD_DIM // 4
D_FF = 4 * D_MODEL
IN_PROJ_WIDTH = 2 * GMLP_WIDTH + ATTN_WIDTH + 2 * KV_WIDTH
N_MOD = 6
EPS = 1e-5
LOG2E = math.log2(math.e)

LANES = 128
SUBLANES = 8
TOKENS_PER_STEP = 512
FF_CHUNK = 1024
FF2_COL_BLOCKS = 4
ADA_COLS = 1536
VMEM_LIMIT_BYTES = 60 * 1024 * 1024

N_SUB = TOKENS_PER_STEP // Q_BLOCK
N_ATT = N_SUB * N_KV_HEADS
N_PAIRS = GMLP_GROUPS // 2

F32 = jnp.float32
BF16 = jnp.bfloat16


def _ada_kernel(c_ref, w_ref, b_ref, o_ref):
    c = c_ref[...]
    c_act = c * (1.0 / (1.0 + jnp.exp(-c)))
    o_ref[...] = jnp.dot(c_act, w_ref[...], preferred_element_type=F32) + b_ref[...]


def _rms_scale(x):
    return lax.rsqrt(jnp.mean(x * x, axis=-1, keepdims=True) + EPS)


def _gelu_tanh(x):
    c = 2.0 * math.sqrt(2.0 / math.pi) * LOG2E
    return x / (1.0 + jnp.exp2(x * (-c - (c * 0.044715) * (x * x))))


def _mixer_stream(s, is_last_in_seq, mod, sinks_ref, x_ref, pos_ref, g_ref, win_ref, wcat_ref, sb_ref,
                  invf_ref, mix_scr, k_scr, v_scr, bd_scr, lhs_scr, sv_scr, sc_scr, p_scr, inv_scr):
    ts = TOKENS_PER_STEP
    x = x_ref[0]
    shift1 = mod[0:1, :]
    scale1 = mod[1:2, :]

    h = ((x * _rms_scale(x)) * (g_ref[...] * (1.0 + scale1)) + shift1).astype(BF16)
    yield
    proj = jnp.dot(h, win_ref[...], preferred_element_type=F32)
    yield

    lane = lax.broadcasted_iota(jnp.int32, (1, LANES), 1)
    left = lane < HEAD_DIM

    t_idx = lax.broadcasted_iota(jnp.int32, (CHUNK, 2 * CHUNK), 0)
    s_idx = lax.broadcasted_iota(jnp.int32, (CHUNK, 2 * CHUNK), 1)
    causal = (s_idx & (CHUNK - 1)) <= t_idx
    w_pairs = [jnp.where(causal, wcat_ref[p], 0.0).astype(BF16) for p in range(N_PAIRS)]
    for c in range(ts // CHUNK):
        rows = slice(c * CHUNK, (c + 1) * CHUNK)
        v = _gelu_tanh(proj[rows, GMLP_WIDTH:2 * GMLP_WIDTH])
        for p in range(N_PAIRS):
            vp = v[:, p * LANES:(p + 1) * LANES]
            bd_scr[c * N_PAIRS + p, 0:CHUNK, :] = jnp.where(left, vp, 0.0).astype(BF16)
            bd_scr[c * N_PAIRS + p, CHUNK:2 * CHUNK, :] = jnp.where(left, 0.0, vp).astype(BF16)
        yield

    ang = invf_ref[...] * pos_ref[0].astype(F32)
    cos8 = jnp.cos(ang)
    sin8 = jnp.sin(ang)
    one8 = jnp.ones_like(ang)
    zero8 = jnp.zeros_like(ang)
    tiles_per_head = HEAD_DIM // SUBLANES
    heads_per_vreg = LANES // HEAD_DIM
    cos_t = jnp.concatenate(([cos8, cos8] + [one8] * (tiles_per_head - 2)) * heads_per_vreg, axis=0).T
    sin_a = jnp.concatenate(([-sin8] + [zero8] * (tiles_per_head - 1)) * heads_per_vreg, axis=0).T
    sin_b = jnp.concatenate(([zero8, sin8] + [zero8] * (tiles_per_head - 2)) * heads_per_vreg, axis=0).T

    def rope(t):
        return (t * cos_t + pltpu.roll(t, LANES - ROT_DIM // 2, 1) * sin_a
                + pltpu.roll(t, ROT_DIM // 2, 1) * sin_b)

    o = 2 * GMLP_WIDTH
    scale = LOG2E / math.sqrt(HEAD_DIM)
    zero = jnp.zeros((), BF16)
    for i in range(ATTN_WIDTH // LANES):
        qp = (rope(proj[:, o + i * LANES:o + (i + 1) * LANES]) * scale).astype(BF16)
        hk, m = divmod(i, 2)
        halves = (jnp.where(left, qp, zero), jnp.where(left, zero, qp))
        for j in range(N_SUB):
            for e in range(2):
                g = 2 * m + e
                lhs_scr[j * N_KV_HEADS + hk, g * Q_BLOCK:(g + 1) * Q_BLOCK, :] = (
                    halves[e][j * Q_BLOCK:(j + 1) * Q_BLOCK])
    yield
    o += ATTN_WIDTH
    k = rope(proj[:, o:o + KV_WIDTH])
    v_att = proj[:, o + KV_WIDTH:o + 2 * KV_WIDTH]
    k_sw = pltpu.roll(k, HEAD_DIM, 1)
    k_scr[0, Q_BLOCK:Q_BLOCK + ts, :] = jnp.where(left, k, k_sw).astype(BF16)
    k_scr[1, Q_BLOCK:Q_BLOCK + ts, :] = jnp.where(left, k_sw, k).astype(BF16)
    v_scr[:, Q_BLOCK:Q_BLOCK + ts] = v_att.T.astype(BF16)
    yield

    for c in range(ts // CHUNK):
        rows = slice(c * CHUNK, (c + 1) * CHUNK)
        for p in range(N_PAIRS):
            sv_scr[rows, p * LANES:(p + 1) * LANES] = jnp.dot(
                w_pairs[p], bd_scr[c * N_PAIRS + p], preferred_element_type=F32)

    def scores(n):
        j, hk = divmod(n, N_KV_HEADS)
        k_win = k_scr[hk, j * Q_BLOCK:(j + 2) * Q_BLOCK, :]
        sc_scr[n] = lax.dot_general(k_win, lhs_scr[n], (((1,), (1,)), ((), ())),
                                    preferred_element_type=F32)

    kj = lax.broadcasted_iota(jnp.int32, (2 * Q_BLOCK, Q_BLOCK), 0)
    qi = lax.broadcasted_iota(jnp.int32, (2 * Q_BLOCK, Q_BLOCK), 1)
    band = (kj > qi) & (kj <= qi + WINDOW)
    bias_band = jnp.where(band, 0.0, -jnp.inf).astype(F32)
    bias_first = jnp.where(band & (kj >= Q_BLOCK), 0.0, -jnp.inf).astype(F32)

    def softmax(n):
        j, hk = divmod(n, N_KV_HEADS)
        bias = jnp.where(s == 0, bias_first, bias_band) if j == 0 else bias_band
        sink = jnp.concatenate(
            [jnp.full((1, Q_BLOCK), sinks_ref[hk * GQA_GROUP + g] * LOG2E, F32)
             for g in range(GQA_GROUP)], axis=1)
        sc = jnp.concatenate(
            [sc_scr[n, :, g * Q_BLOCK:(g + 1) * Q_BLOCK] + bias for g in range(GQA_GROUP)], axis=1)
        mx = jnp.maximum(jnp.max(sc, axis=0, keepdims=True), sink)
        p = jnp.exp2(sc - mx)
        denom = jnp.sum(p, axis=0, keepdims=True) + jnp.exp2(sink - mx)
        p_scr[n] = p.astype(BF16)
        inv_scr[n] = 1.0 / denom

    def values(n):
        j, hk = divmod(n, N_KV_HEADS)
        rows = slice(j * Q_BLOCK, (j + 1) * Q_BLOCK)
        v_win = v_scr[hk * HEAD_DIM:(hk + 1) * HEAD_DIM, j * Q_BLOCK:(j + 2) * Q_BLOCK]
        pv = jnp.dot(v_win, p_scr[n], preferred_element_type=F32) * inv_scr[n]
        for m in range(2):
            pair = jnp.concatenate([pv[:, (2 * m) * Q_BLOCK:(2 * m + 1) * Q_BLOCK],
                                    pv[:, (2 * m + 1) * Q_BLOCK:(2 * m + 2) * Q_BLOCK]], axis=0)
            col = GMLP_WIDTH + (2 * hk + m) * LANES
            mix_scr[rows, col:col + LANES] = pair.T.astype(BF16)

    def gate(c):
        rows = slice(c * CHUNK, (c + 1) * CHUNK)
        u = _gelu_tanh(proj[rows, 0:GMLP_WIDTH])
        mix_scr[rows, 0:GMLP_WIDTH] = (u * (sv_scr[rows, :] + sb_ref[...])).astype(BF16)

    scores(0); softmax(0); scores(1); softmax(1)
    for c in range(ts // CHUNK):
        gate(c)
    yield
    scores(2); softmax(2); scores(3); softmax(3)
    yield
    values(0); values(1); scores(4); softmax(4); scores(5); softmax(5)
    yield
    values(2); values(3); scores(6); softmax(6); scores(7); softmax(7)
    yield
    values(4); values(5)
    yield
    values(6); values(7)

    k_last = k_scr[:, ts:ts + Q_BLOCK, :]
    v_last = v_scr[:, ts:ts + Q_BLOCK]
    k_scr[:, 0:Q_BLOCK, :] = jnp.where(is_last_in_seq, jnp.zeros_like(k_last), k_last)
    v_scr[:, 0:Q_BLOCK] = jnp.where(is_last_in_seq, jnp.zeros_like(v_last), v_last)


def _ffn_stream(mod, x_ref, mix_scr, wout_ref, g_ref, w1_ref, w2_ref, gf_ref, o_ref, hid_scr, y_scr):
    gate1 = mod[2:3, :]
    shift2 = mod[3:4, :]
    scale2 = mod[4:5, :]
    gate2 = mod[5:6, :]
    mix = jnp.dot(mix_scr[...], wout_ref[...], preferred_element_type=F32)
    yield
    x = x_ref[0] + gate1 * mix
    h = ((x * _rms_scale(x)) * (g_ref[...] * (1.0 + scale2)) + shift2).astype(BF16)
    yield

    for c in range(D_FF // FF_CHUNK):
        cols = slice(c * FF_CHUNK, (c + 1) * FF_CHUNK)
        a = jnp.dot(h, w1_ref[:, cols], preferred_element_type=F32)
        a = jnp.maximum(a, 0.0)
        hid_scr[:, cols] = (a * a).astype(BF16)
        yield
    wcol = D_MODEL // FF2_COL_BLOCKS
    gf = gf_ref[...]

    def finish(rows, cols, ff):
        y = x[rows, cols] + gate2[:, cols] * ff
        y_scr[rows, cols] = y * gf[:, cols]
        sq = y * y
        return sum(sq[:, l * LANES:(l + 1) * LANES] for l in range(wcol // LANES))

    ssq = None
    for q in range(FF2_COL_BLOCKS - 1):
        cols = slice(q * wcol, (q + 1) * wcol)
        ff = jnp.dot(hid_scr[...], w2_ref[:, cols], preferred_element_type=F32)
        part = finish(slice(None), cols, ff)
        ssq = part if ssq is None else ssq + part
        yield
    cols = slice((FF2_COL_BLOCKS - 1) * wcol, FF2_COL_BLOCKS * wcol)
    half = TOKENS_PER_STEP // 2
    for r in range(2):
        rows = slice(r * half, (r + 1) * half)
        ff = jnp.dot(hid_scr[rows, :], w2_ref[:, cols], preferred_element_type=F32)
        tot = ssq[rows] + finish(rows, cols, ff)
        scale = lax.rsqrt(jnp.sum(tot, axis=-1, keepdims=True) * (1.0 / D_MODEL) + EPS)
        o_ref[0, rows, :] = y_scr[rows, :] * scale
        yield


def _block_kernel(sinks_ref, xa_ref, xb_ref, pos_ref, mod_ref, gm_ref, win_ref, wcat_ref, sb_ref,
                  invf_ref, wout_ref, gffn_ref, w1_ref, w2_ref, gf_ref, o_ref,
                  mix_scr, hid_scr, k_scr, v_scr, bd_scr, lhs_scr, sv_scr, sc_scr, p_scr, inv_scr, y_scr,
                  *, n_tiles, tiles_per_seq):
    i = pl.program_id(0)
    tile_a = jnp.minimum(i, n_tiles - 1)
    tile_b = jnp.maximum(i - 1, 0)

    def ffn_stream():
        return _ffn_stream(mod_ref[tile_b // tiles_per_seq], xb_ref, mix_scr, wout_ref, gffn_ref, w1_ref,
                           w2_ref, gf_ref, o_ref, hid_scr, y_scr)

    def mixer_stream():
        s_a = tile_a % tiles_per_seq
        return _mixer_stream(s_a, s_a == tiles_per_seq - 1, mod_ref[tile_a // tiles_per_seq], sinks_ref,
                             xa_ref, pos_ref, gm_ref, win_ref, wcat_ref, sb_ref, invf_ref, mix_scr,
                             k_scr, v_scr, bd_scr, lhs_scr, sv_scr, sc_scr, p_scr, inv_scr)

    @pl.when(i == 0)
    def _():
        k_scr[:, 0:Q_BLOCK, :] = jnp.zeros((N_KV_HEADS, Q_BLOCK, LANES), BF16)
        v_scr[:, 0:Q_BLOCK] = jnp.zeros((KV_WIDTH, Q_BLOCK), BF16)
        for _ in mixer_stream():
            pass

    @pl.when(i == n_tiles)
    def _():
        for _ in ffn_stream():
            pass

    @pl.when(jnp.logical_and(i > 0, i < n_tiles))
    def _():
        ffn = ffn_stream()
        mixer = mixer_stream()
        order = ("ab"
                 "ab"
                 "baabaabaa"
                 "ababababab"
                 "ab")
        streams = {"a": mixer, "b": ffn}
        for name in order:
            next(streams[name], None)
        for stream in (mixer, ffn):
            for _ in stream:
                pass


def _inv_freq_column():
    inv_freq = ROPE_THETA ** (-np.arange(0, ROT_DIM, 2, dtype=np.float32) / ROT_DIM)
    return inv_freq.astype(np.float32)[:, None]


def _const_spec(shape):
    zeros = (0,) * len(shape)
    return pl.BlockSpec(shape, lambda *_: zeros, pipeline_mode=pl.Buffered(1))


def kernel(x, c, positions, w_ada, b_ada, g_mix, w_in, w_spatial, b_spatial, sinks, w_out, g_ffn,
           w_ff1, w_ff2, g_final):
    B, S, D = x.shape
    ts = TOKENS_PER_STEP
    assert D == D_MODEL and S % ts == 0 and w_ada.shape[0] == 1

    n_ada = N_MOD * D
    mod = pl.pallas_call(
        _ada_kernel,
        grid=(n_ada // ADA_COLS,),
        in_specs=[pl.BlockSpec((B, D), lambda i: (0, 0)),
                  pl.BlockSpec((D, ADA_COLS), lambda i: (0, i)),
                  pl.BlockSpec((1, ADA_COLS), lambda i: (0, i))],
        out_specs=pl.BlockSpec((B, ADA_COLS), lambda i: (0, i)),
        out_shape=jax.ShapeDtypeStruct((B, n_ada), F32),
        compiler_params=pltpu.CompilerParams(dimension_semantics=("arbitrary",),
                                             vmem_limit_bytes=VMEM_LIMIT_BYTES),
        name="ada_mod",
    )(c, w_ada[0], b_ada[0][None])
    mod = mod.reshape(B, N_MOD, D)

    ws = w_spatial[0]
    wcat = ws.reshape(N_PAIRS, 2, CHUNK, CHUNK).transpose(0, 2, 1, 3).reshape(
        N_PAIRS, CHUNK, 2 * CHUNK)
    sbias = jnp.repeat(b_spatial[0].T, HEAD_DIM, axis=1)
    pos3 = positions.reshape(B, 1, S)

    tiles_per_seq = S // ts
    n_tiles = B * tiles_per_seq

    def tile_a(i):
        return jnp.minimum(i, n_tiles - 1)

    def tile_b(i):
        return jnp.maximum(i - 1, 0)

    def tok_spec(tile):
        return pl.BlockSpec((1, ts, D), lambda i: (tile(i) // tiles_per_seq, tile(i) % tiles_per_seq, 0))

    return pl.pallas_call(
        functools.partial(_block_kernel, n_tiles=n_tiles, tiles_per_seq=tiles_per_seq),
        grid=(n_tiles + 1,),
        in_specs=[pl.BlockSpec(memory_space=pltpu.SMEM),
                  tok_spec(tile_a),
                  tok_spec(tile_b),
                  pl.BlockSpec((1, 1, ts),
                               lambda i: (tile_a(i) // tiles_per_seq, 0, tile_a(i) % tiles_per_seq)),
                  _const_spec((B, N_MOD, D)),
                  _const_spec((1, D)),
                  _const_spec((D, IN_PROJ_WIDTH)),
                  _const_spec((N_PAIRS, CHUNK, 2 * CHUNK)),
                  _const_spec((CHUNK, GMLP_WIDTH)),
                  _const_spec((ROT_DIM // 2, 1)),
                  _const_spec((D, D)),
                  _const_spec((1, D)),
                  _const_spec((D, D_FF)),
                  _const_spec((D_FF, D)),
                  _const_spec((1, D))],
        out_specs=tok_spec(tile_b),
        out_shape=jax.ShapeDtypeStruct((B, S, D), F32),
        scratch_shapes=[pltpu.VMEM((ts, D), BF16),
                        pltpu.VMEM((ts, D_FF), BF16),
                        pltpu.VMEM((N_KV_HEADS, Q_BLOCK + ts, LANES), BF16),
                        pltpu.VMEM((KV_WIDTH, Q_BLOCK + ts), BF16),
                        pltpu.VMEM((N_SUB * N_PAIRS, 2 * CHUNK, LANES), BF16),
                        pltpu.VMEM((N_ATT, GQA_GROUP * Q_BLOCK, LANES), BF16),
                        pltpu.VMEM((ts, GMLP_WIDTH), F32),
                        pltpu.VMEM((N_ATT, 2 * Q_BLOCK, GQA_GROUP * Q_BLOCK), F32),
                        pltpu.VMEM((N_ATT, 2 * Q_BLOCK, GQA_GROUP * Q_BLOCK), BF16),
                        pltpu.VMEM((N_ATT, 1, GQA_GROUP * Q_BLOCK), F32),
                        pltpu.VMEM((ts, D), F32)],
        compiler_params=pltpu.CompilerParams(dimension_semantics=("arbitrary",),
                                             vmem_limit_bytes=VMEM_LIMIT_BYTES),
        name="decoder_block",
    )(sinks[0], x, x, pos3, mod, g_mix, w_in[0].astype(BF16), wcat, sbias,
      jnp.asarray(_inv_freq_column()), w_out[0].astype(BF16), g_ffn,
      w_ff1[0].astype(BF16), w_ff2[0].astype(BF16), g_final[None])
```

```python
import functools
import math

import numpy as np
import jax
import jax.numpy as jnp
from jax import lax
from jax.experimental import pallas as pl
from jax.experimental.pallas import tpu as pltpu

D_MODEL = 1024
HEAD_DIM = 64
GMLP_GROUPS = 8
GMLP_WIDTH = GMLP_GROUPS * HEAD_DIM
CHUNK = 128
N_Q_HEADS = 8
N_KV_HEADS = 2
GQA_GROUP = N_Q_HEADS // N_KV_HEADS
ATTN_WIDTH = N_Q_HEADS * HEAD_DIM
KV_WIDTH = N_KV_HEADS * HEAD_DIM
WINDOW = 128
Q_BLOCK = 128
ROPE_THETA = 500000.0
ROT_DIM = HEAD_DIM // 4
D_FF = 4 * D_MODEL
IN_PROJ_WIDTH = 2 * GMLP_WIDTH + ATTN_WIDTH + 2 * KV_WIDTH
N_MOD = 6
EPS = 1e-5
LOG2E = math.log2(math.e)

LANES = 128
SUBLANES = 8
TOKENS_PER_STEP = 512
FF_CHUNK = 1024
FF2_COL_BLOCKS = 4
VMEM_LIMIT_BYTES = 60 * 1024 * 1024

N_SUB = TOKENS_PER_STEP // Q_BLOCK
N_ATT = N_SUB * N_KV_HEADS
N_PAIRS = GMLP_GROUPS // 2

F32 = jnp.float32
BF16 = jnp.bfloat16


def _ada_kernel(c_ref, w_ref, b_ref, o_ref):
    c = c_ref[...]
    c_act = c * (1.0 / (1.0 + jnp.exp(-c)))
    o_ref[0] = jnp.dot(c_act, w_ref[...], preferred_element_type=F32) + b_ref[...]


def _rms_scale(x):
    return lax.rsqrt(jnp.mean(x * x, axis=-1, keepdims=True) + EPS)


def _gelu_tanh(x):
    c = 2.0 * math.sqrt(2.0 / math.pi) * LOG2E
    return x / (1.0 + jnp.exp2(x * (-c - (c * 0.044715) * (x * x))))


def _mixer_stream(s, is_last_in_seq, mod, sinks_ref, x_ref, pos_row, g_ref, win_ref, ws_ref, sb_ref,
                  invf_ref, mix_scr, k_scr, v_scr, bd_scr, lhs_scr, sv_scr, sc_scr, p_scr, inv_scr):
    ts = TOKENS_PER_STEP
    x = x_ref[0]
    shift1, scale1 = mod[0], mod[1]

    h = ((x * _rms_scale(x)) * (g_ref[...] * (1.0 + scale1)) + shift1).astype(BF16)
    yield
    proj = jnp.dot(h, win_ref[...], preferred_element_type=F32)
    yield

    lane = lax.broadcasted_iota(jnp.int32, (1, LANES), 1)
    left = lane < HEAD_DIM

    t_idx = lax.broadcasted_iota(jnp.int32, (CHUNK, 2 * CHUNK), 0)
    s_idx = lax.broadcasted_iota(jnp.int32, (CHUNK, 2 * CHUNK), 1)
    causal = (s_idx & (CHUNK - 1)) <= t_idx
    w_pairs = [jnp.where(causal, jnp.concatenate([ws_ref[2 * p], ws_ref[2 * p + 1]], axis=1), 0.0).astype(BF16)
               for p in range(N_PAIRS)]
    for c in range(ts // CHUNK):
        rows = slice(c * CHUNK, (c + 1) * CHUNK)
        v = _gelu_tanh(proj[rows, GMLP_WIDTH:2 * GMLP_WIDTH])
        for p in range(N_PAIRS):
            vp = v[:, p * LANES:(p + 1) * LANES]
            bd_scr[c * N_PAIRS + p, 0:CHUNK, :] = jnp.where(left, vp, 0.0).astype(BF16)
            bd_scr[c * N_PAIRS + p, CHUNK:2 * CHUNK, :] = jnp.where(left, 0.0, vp).astype(BF16)
        yield

    ang = invf_ref[...] * pos_row.astype(F32)
    cos8 = jnp.cos(ang)
    sin8 = jnp.sin(ang)
    one8 = jnp.ones_like(ang)
    zero8 = jnp.zeros_like(ang)
    tiles_per_head = HEAD_DIM // SUBLANES
    heads_per_vreg = LANES // HEAD_DIM
    cos_t = jnp.concatenate(([cos8, cos8] + [one8] * (tiles_per_head - 2)) * heads_per_vreg, axis=0).T
    sin_a = jnp.concatenate(([-sin8] + [zero8] * (tiles_per_head - 1)) * heads_per_vreg, axis=0).T
    sin_b = jnp.concatenate(([zero8, sin8] + [zero8] * (tiles_per_head - 2)) * heads_per_vreg, axis=0).T

    def rope(t):
        return (t * cos_t + pltpu.roll(t, LANES - ROT_DIM // 2, 1) * sin_a
                + pltpu.roll(t, ROT_DIM // 2, 1) * sin_b)

    o = 2 * GMLP_WIDTH
    scale = LOG2E / math.sqrt(HEAD_DIM)
    zero = jnp.zeros((), BF16)
    for i in range(ATTN_WIDTH // LANES):
        qp = (rope(proj[:, o + i * LANES:o + (i + 1) * LANES]) * scale).astype(BF16)
        hk, m = divmod(i, 2)
        halves = (jnp.where(left, qp, zero), jnp.where(left, zero, qp))
        for j in range(N_SUB):
            for e in range(2):
                g = 2 * m + e
                lhs_scr[j * N_KV_HEADS + hk, g * Q_BLOCK:(g + 1) * Q_BLOCK, :] = (
                    halves[e][j * Q_BLOCK:(j + 1) * Q_BLOCK])
    yield
    o += ATTN_WIDTH
    k = rope(proj[:, o:o + KV_WIDTH])
    v_att = proj[:, o + KV_WIDTH:o + 2 * KV_WIDTH]
    k_sw = pltpu.roll(k, HEAD_DIM, 1)
    k_scr[0, Q_BLOCK:Q_BLOCK + ts, :] = jnp.where(left, k, k_sw).astype(BF16)
    k_scr[1, Q_BLOCK:Q_BLOCK + ts, :] = jnp.where(left, k_sw, k).astype(BF16)
    v_scr[:, Q_BLOCK:Q_BLOCK + ts] = v_att.T.astype(BF16)
    yield

    for c in range(ts // CHUNK):
        rows = slice(c * CHUNK, (c + 1) * CHUNK)
        for p in range(N_PAIRS):
            sv_scr[rows, p * LANES:(p + 1) * LANES] = jnp.dot(
                w_pairs[p], bd_scr[c * N_PAIRS + p], preferred_element_type=F32)

    def scores(n):
        j, hk = divmod(n, N_KV_HEADS)
        k_win = k_scr[hk, j * Q_BLOCK:(j + 2) * Q_BLOCK, :]
        sc_scr[n] = lax.dot_general(k_win, lhs_scr[n], (((1,), (1,)), ((), ())),
                                    preferred_element_type=F32)

    kj = lax.broadcasted_iota(jnp.int32, (2 * Q_BLOCK, Q_BLOCK), 0)
    qi = lax.broadcasted_iota(jnp.int32, (2 * Q_BLOCK, Q_BLOCK), 1)
    band = (kj > qi) & (kj <= qi + WINDOW)
    bias_band = jnp.where(band, 0.0, -jnp.inf).astype(F32)
    bias_first = jnp.where(band & (kj >= Q_BLOCK), 0.0, -jnp.inf).astype(F32)

    def softmax(n):
        j, hk = divmod(n, N_KV_HEADS)
        bias = jnp.where(s == 0, bias_first, bias_band) if j == 0 else bias_band
        sink = jnp.concatenate(
            [jnp.full((1, Q_BLOCK), sinks_ref[hk * GQA_GROUP + g] * LOG2E, F32)
             for g in range(GQA_GROUP)], axis=1)
        sc = jnp.concatenate(
            [sc_scr[n, :, g * Q_BLOCK:(g + 1) * Q_BLOCK] + bias for g in range(GQA_GROUP)], axis=1)
        mx = jnp.maximum(jnp.max(sc, axis=0, keepdims=True), sink)
        p = jnp.exp2(sc - mx)
        denom = jnp.sum(p, axis=0, keepdims=True) + jnp.exp2(sink - mx)
        p_scr[n] = p.astype(BF16)
        inv_scr[n] = 1.0 / denom

    def values(n):
        j, hk = divmod(n, N_KV_HEADS)
        rows = slice(j * Q_BLOCK, (j + 1) * Q_BLOCK)
        v_win = v_scr[hk * HEAD_DIM:(hk + 1) * HEAD_DIM, j * Q_BLOCK:(j + 2) * Q_BLOCK]
        pv = jnp.dot(v_win, p_scr[n], preferred_element_type=F32) * inv_scr[n]
        for m in range(2):
            pair = jnp.concatenate([pv[:, (2 * m) * Q_BLOCK:(2 * m + 1) * Q_BLOCK],
                                    pv[:, (2 * m + 1) * Q_BLOCK:(2 * m + 2) * Q_BLOCK]], axis=0)
            col = GMLP_WIDTH + (2 * hk + m) * LANES
            mix_scr[rows, col:col + LANES] = pair.T.astype(BF16)

    def gate(c):
        rows = slice(c * CHUNK, (c + 1) * CHUNK)
        u = _gelu_tanh(proj[rows, 0:GMLP_WIDTH])
        mix_scr[rows, 0:GMLP_WIDTH] = (u * (sv_scr[rows, :] + sb_ref[...])).astype(BF16)

    scores(0); softmax(0); scores(1); softmax(1)
    for c in range(ts // CHUNK):
        gate(c)
    yield
    scores(2); softmax(2); scores(3); softmax(3)
    yield
    values(0); values(1); scores(4); softmax(4); scores(5); softmax(5)
    yield
    values(2); values(3); scores(6); softmax(6); scores(7); softmax(7)
    yield
    values(4); values(5)
    yield
    values(6); values(7)

    k_last = k_scr[:, ts:ts + Q_BLOCK, :]
    v_last = v_scr[:, ts:ts + Q_BLOCK]
    k_scr[:, 0:Q_BLOCK, :] = jnp.where(is_last_in_seq, jnp.zeros_like(k_last), k_last)
    v_scr[:, 0:Q_BLOCK] = jnp.where(is_last_in_seq, jnp.zeros_like(v_last), v_last)


def _ffn_stream(mod, x_ref, mix_scr, wout_ref, g_ref, w1_ref, w2_ref, gf_ref, o_ref, hid_scr, y_scr):
    gate1, shift2, scale2, gate2 = mod[2], mod[3], mod[4], mod[5]
    mix = jnp.dot(mix_scr[...], wout_ref[...], preferred_element_type=F32)
    yield
    x = x_ref[0] + gate1 * mix
    h = ((x * _rms_scale(x)) * (g_ref[...] * (1.0 + scale2)) + shift2).astype(BF16)
    yield

    for c in range(D_FF // FF_CHUNK):
        cols = slice(c * FF_CHUNK, (c + 1) * FF_CHUNK)
        a = jnp.dot(h, w1_ref[:, cols], preferred_element_type=F32)
        a = jnp.maximum(a, 0.0)
        hid_scr[:, cols] = (a * a).astype(BF16)
        yield
    wcol = D_MODEL // FF2_COL_BLOCKS
    gf = gf_ref[...]

    def finish(rows, cols, ff):
        y = x[rows, cols] + gate2[:, cols] * ff
        y_scr[rows, cols] = y * gf[:, cols]
        sq = y * y
        return sum(sq[:, l * LANES:(l + 1) * LANES] for l in range(wcol // LANES))

    ssq = None
    for q in range(FF2_COL_BLOCKS - 1):
        cols = slice(q * wcol, (q + 1) * wcol)
        ff = jnp.dot(hid_scr[...], w2_ref[:, cols], preferred_element_type=F32)
        part = finish(slice(None), cols, ff)
        ssq = part if ssq is None else ssq + part
        yield
    cols = slice((FF2_COL_BLOCKS - 1) * wcol, FF2_COL_BLOCKS * wcol)
    half = TOKENS_PER_STEP // 2
    for r in range(2):
        rows = slice(r * half, (r + 1) * half)
        ff = jnp.dot(hid_scr[rows, :], w2_ref[:, cols], preferred_element_type=F32)
        tot = ssq[rows] + finish(rows, cols, ff)
        scale = lax.rsqrt(jnp.sum(tot, axis=-1, keepdims=True) * (1.0 / D_MODEL) + EPS)
        o_ref[0, rows, :] = y_scr[rows, :] * scale
        yield


def _block_kernel(sinks_ref, xa_ref, xb_ref, pos_ref, mod_ref, gm_ref, win_ref, ws_ref, sb_ref,
                  invf_ref, wout_ref, gffn_ref, w1_ref, w2_ref, gf_ref, o_ref,
                  mix_scr, hid_scr, k_scr, v_scr, bd_scr, lhs_scr, sv_scr, sc_scr, p_scr, inv_scr, y_scr,
                  *, n_tiles, tiles_per_seq):
    i = pl.program_id(0)
    tile_a = jnp.minimum(i, n_tiles - 1)
    tile_b = jnp.maximum(i - 1, 0)

    def mod_rows(tile):
        return [mod_ref[k, pl.ds(tile // tiles_per_seq, 1), :] for k in range(N_MOD)]

    def ffn_stream():
        return _ffn_stream(mod_rows(tile_b), xb_ref, mix_scr, wout_ref, gffn_ref, w1_ref,
                           w2_ref, gf_ref, o_ref, hid_scr, y_scr)

    def mixer_stream():
        s_a = tile_a % tiles_per_seq
        pos_row = pos_ref[pl.ds(tile_a // tiles_per_seq, 1), :]
        return _mixer_stream(s_a, s_a == tiles_per_seq - 1, mod_rows(tile_a), sinks_ref,
                             xa_ref, pos_row, gm_ref, win_ref, ws_ref, sb_ref, invf_ref, mix_scr,
                             k_scr, v_scr, bd_scr, lhs_scr, sv_scr, sc_scr, p_scr, inv_scr)

    @pl.when(i == 0)
    def _():
        k_scr[:, 0:Q_BLOCK, :] = jnp.zeros((N_KV_HEADS, Q_BLOCK, LANES), BF16)
        v_scr[:, 0:Q_BLOCK] = jnp.zeros((KV_WIDTH, Q_BLOCK), BF16)
        for _ in mixer_stream():
            pass

    @pl.when(i == n_tiles)
    def _():
        for _ in ffn_stream():
            pass

    @pl.when(jnp.logical_and(i > 0, i < n_tiles))
    def _():
        ffn = ffn_stream()
        mixer = mixer_stream()
        order = ("ab"
                 "ab"
                 "baabaabaa"
                 "ababababab"
                 "ab")
        streams = {"a": mixer, "b": ffn}
        for name in order:
            next(streams[name], None)
        for stream in (mixer, ffn):
            for _ in stream:
                pass


def _inv_freq_column():
    inv_freq = ROPE_THETA ** (-np.arange(0, ROT_DIM, 2, dtype=np.float32) / ROT_DIM)
    return inv_freq.astype(np.float32)[:, None]


def _const_spec(shape):
    zeros = (0,) * len(shape)
    return pl.BlockSpec(shape, lambda *_: zeros, pipeline_mode=pl.Buffered(1))


def kernel(x, c, positions, w_ada, b_ada, g_mix, w_in, w_spatial, b_spatial, sinks, w_out, g_ffn,
           w_ff1, w_ff2, g_final):
    B, S, D = x.shape
    ts = TOKENS_PER_STEP
    assert D == D_MODEL and S % ts == 0 and w_ada.shape[0] == 1

    mod = pl.pallas_call(
        _ada_kernel,
        grid=(N_MOD,),
        in_specs=[pl.BlockSpec((B, D), lambda i: (0, 0)),
                  pl.BlockSpec((D, D), lambda i: (0, i)),
                  pl.BlockSpec((1, D), lambda i: (0, i))],
        out_specs=pl.BlockSpec((1, B, D), lambda i: (i, 0, 0)),
        out_shape=jax.ShapeDtypeStruct((N_MOD, B, D), F32),
        compiler_params=pltpu.CompilerParams(dimension_semantics=("arbitrary",),
                                             vmem_limit_bytes=VMEM_LIMIT_BYTES),
        name="ada_mod",
    )(c, w_ada[0], b_ada)

    sbias = jnp.repeat(b_spatial[0].T, HEAD_DIM, axis=1)

    tiles_per_seq = S // ts
    n_tiles = B * tiles_per_seq

    def tile_a(i):
        return jnp.minimum(i, n_tiles - 1)

    def tile_b(i):
        return jnp.maximum(i - 1, 0)

    def tok_spec(tile):
        return pl.BlockSpec((1, ts, D), lambda i: (tile(i) // tiles_per_seq, tile(i) % tiles_per_seq, 0))

    return pl.pallas_call(
        functools.partial(_block_kernel, n_tiles=n_tiles, tiles_per_seq=tiles_per_seq),
        grid=(n_tiles + 1,),
        in_specs=[pl.BlockSpec(memory_space=pltpu.SMEM),
                  tok_spec(tile_a),
                  tok_spec(tile_b),
                  pl.BlockSpec((B, ts), lambda i: (0, tile_a(i) % tiles_per_seq)),
                  _const_spec((N_MOD, B, D)),
                  _const_spec((1, D)),
                  _const_spec((D, IN_PROJ_WIDTH)),
                  _const_spec((GMLP_GROUPS, CHUNK, CHUNK)),
                  _const_spec((CHUNK, GMLP_WIDTH)),
                  _const_spec((ROT_DIM // 2, 1)),
                  _const_spec((D, D)),
                  _const_spec((1, D)),
                  _const_spec((D, D_FF)),
                  _const_spec((D_FF, D)),
                  _const_spec((1, D))],
        out_specs=tok_spec(tile_b),
        out_shape=jax.ShapeDtypeStruct((B, S, D), F32),
        scratch_shapes=[pltpu.VMEM((ts, D), BF16),
                        pltpu.VMEM((ts, D_FF), BF16),
                        pltpu.VMEM((N_KV_HEADS, Q_BLOCK + ts, LANES), BF16),
                        pltpu.VMEM((KV_WIDTH, Q_BLOCK + ts), BF16),
                        pltpu.VMEM((N_SUB * N_PAIRS, 2 * CHUNK, LANES), BF16),
                        pltpu.VMEM((N_ATT, GQA_GROUP * Q_BLOCK, LANES), BF16),
                        pltpu.VMEM((ts, GMLP_WIDTH), F32),
                        pltpu.VMEM((N_ATT, 2 * Q_BLOCK, GQA_GROUP * Q_BLOCK), F32),
                        pltpu.VMEM((N_ATT, 2 * Q_BLOCK, GQA_GROUP * Q_BLOCK), BF16),
                        pltpu.VMEM((N_ATT, 1, GQA_GROUP * Q_BLOCK), F32),
                        pltpu.VMEM((ts, D), F32)],
        compiler_params=pltpu.CompilerParams(dimension_semantics=("arbitrary",),
                                             vmem_limit_bytes=VMEM_LIMIT_BYTES),
        name="decoder_block",
    )(sinks[0], x, x, positions, mod, g_mix, w_in[0].astype(BF16), w_spatial[0], sbias,
      jnp.asarray(_inv_freq_column()), w_out[0].astype(BF16), g_ffn,
      w_ff1[0].astype(BF16), w_ff2[0].astype(BF16), g_final[None])
```

```python
import functools
import math

import numpy as np
import jax
import jax.numpy as jnp
from jax import lax
from jax.experimental import pallas as pl
from jax.experimental.pallas import tpu as pltpu

D_MODEL = 1024
HEAD_DIM = 64
GMLP_GROUPS = 8
GMLP_WIDTH = GMLP_GROUPS * HEAD_DIM
CHUNK = 128
N_Q_HEADS = 8
N_KV_HEADS = 2
GQA_GROUP = N_Q_HEADS // N_KV_HEADS
ATTN_WIDTH = N_Q_HEADS * HEAD_DIM
KV_WIDTH = N_KV_HEADS * HEAD_DIM
WINDOW = 128
Q_BLOCK = 128
ROPE_THETA = 500000.0
ROT_DIM = HEAD_DIM // 4
D_FF = 4 * D_MODEL
IN_PROJ_WIDTH = 2 * GMLP_WIDTH + ATTN_WIDTH + 2 * KV_WIDTH
N_MOD = 6
EPS = 1e-5
LOG2E = math.log2(math.e)

LANES = 128
SUBLANES = 8
TOKENS_PER_STEP = 512
FF_CHUNK = 1024
FF2_COL_BLOCKS = 4
VMEM_LIMIT_BYTES = 60 * 1024 * 1024

N_SUB = TOKENS_PER_STEP // Q_BLOCK
N_ATT = N_SUB * N_KV_HEADS
N_PAIRS = GMLP_GROUPS // 2

F32 = jnp.float32
BF16 = jnp.bfloat16


def _ada_kernel(c_ref, w_ref, b_ref, o_ref):
    c = c_ref[...]
    c_act = c * (1.0 / (1.0 + jnp.exp(-c)))
    o_ref[0] = jnp.dot(c_act, w_ref[...], preferred_element_type=F32) + b_ref[...]


def _rms_scale(x):
    return lax.rsqrt(jnp.mean(x * x, axis=-1, keepdims=True) + EPS)


def _gelu_tanh(x):
    c = 2.0 * math.sqrt(2.0 / math.pi) * LOG2E
    return x / (1.0 + jnp.exp2(x * (-c - (c * 0.044715) * (x * x))))


def _mixer_stream(s, is_last_in_seq, mod, sinks_ref, x_ref, pos_row, g_ref, win_ref, ws_ref, sb_ref,
                  invf_ref, mix_scr, k_scr, v_scr, bd_scr, lhs_scr, sv_scr, sc_scr, p_scr, inv_scr):
    ts = TOKENS_PER_STEP
    x = x_ref[0]
    shift1, scale1 = mod[0], mod[1]

    h = ((x * _rms_scale(x)) * (g_ref[...] * (1.0 + scale1)) + shift1).astype(BF16)
    yield
    proj = jnp.dot(h, win_ref[...], preferred_element_type=F32)
    yield

    lane = lax.broadcasted_iota(jnp.int32, (1, LANES), 1)
    left = lane < HEAD_DIM

    t_idx = lax.broadcasted_iota(jnp.int32, (CHUNK, 2 * CHUNK), 0)
    s_idx = lax.broadcasted_iota(jnp.int32, (CHUNK, 2 * CHUNK), 1)
    causal = (s_idx & (CHUNK - 1)) <= t_idx
    w_pairs = [jnp.where(causal, jnp.concatenate([ws_ref[2 * p], ws_ref[2 * p + 1]], axis=1), 0.0).astype(BF16)
               for p in range(N_PAIRS)]
    for c in range(ts // CHUNK):
        rows = slice(c * CHUNK, (c + 1) * CHUNK)
        v = _gelu_tanh(proj[rows, GMLP_WIDTH:2 * GMLP_WIDTH])
        for p in range(N_PAIRS):
            vp = v[:, p * LANES:(p + 1) * LANES]
            bd_scr[c * N_PAIRS + p, 0:CHUNK, :] = jnp.where(left, vp, 0.0).astype(BF16)
            bd_scr[c * N_PAIRS + p, CHUNK:2 * CHUNK, :] = jnp.where(left, 0.0, vp).astype(BF16)
        yield

    ang = invf_ref[...] * pos_row.astype(F32)
    cos8 = jnp.cos(ang)
    sin8 = jnp.sin(ang)
    one8 = jnp.ones_like(ang)
    zero8 = jnp.zeros_like(ang)
    tiles_per_head = HEAD_DIM // SUBLANES
    heads_per_vreg = LANES // HEAD_DIM
    cos_t = jnp.concatenate(([cos8, cos8] + [one8] * (tiles_per_head - 2)) * heads_per_vreg, axis=0).T
    sin_a = jnp.concatenate(([-sin8] + [zero8] * (tiles_per_head - 1)) * heads_per_vreg, axis=0).T
    sin_b = jnp.concatenate(([zero8, sin8] + [zero8] * (tiles_per_head - 2)) * heads_per_vreg, axis=0).T

    def rope(t):
        return (t * cos_t + pltpu.roll(t, LANES - ROT_DIM // 2, 1) * sin_a
                + pltpu.roll(t, ROT_DIM // 2, 1) * sin_b)

    o = 2 * GMLP_WIDTH
    scale = LOG2E / math.sqrt(HEAD_DIM)
    zero = jnp.zeros((), BF16)
    for i in range(ATTN_WIDTH // LANES):
        qp = (rope(proj[:, o + i * LANES:o + (i + 1) * LANES]) * scale).astype(BF16)
        hk, m = divmod(i, 2)
        halves = (jnp.where(left, qp, zero), jnp.where(left, zero, qp))
        for j in range(N_SUB):
            for e in range(2):
                g = 2 * m + e
                lhs_scr[j * N_KV_HEADS + hk, g * Q_BLOCK:(g + 1) * Q_BLOCK, :] = (
                    halves[e][j * Q_BLOCK:(j + 1) * Q_BLOCK])
    yield
    o += ATTN_WIDTH
    k = rope(proj[:, o:o + KV_WIDTH])
    v_att = proj[:, o + KV_WIDTH:o + 2 * KV_WIDTH]
    k_sw = pltpu.roll(k, HEAD_DIM, 1)
    k_scr[0, Q_BLOCK:Q_BLOCK + ts, :] = jnp.where(left, k, k_sw).astype(BF16)
    k_scr[1, Q_BLOCK:Q_BLOCK + ts, :] = jnp.where(left, k_sw, k).astype(BF16)
    v_scr[:, Q_BLOCK:Q_BLOCK + ts] = v_att.T.astype(BF16)
    yield

    for c in range(ts // CHUNK):
        rows = slice(c * CHUNK, (c + 1) * CHUNK)
        for p in range(N_PAIRS):
            sv_scr[rows, p * LANES:(p + 1) * LANES] = jnp.dot(
                w_pairs[p], bd_scr[c * N_PAIRS + p], preferred_element_type=F32)

    def scores(n):
        j, hk = divmod(n, N_KV_HEADS)
        k_win = k_scr[hk, j * Q_BLOCK:(j + 2) * Q_BLOCK, :]
        sc_scr[n] = lax.dot_general(k_win, lhs_scr[n], (((1,), (1,)), ((), ())),
                                    preferred_element_type=F32)

    kj = lax.broadcasted_iota(jnp.int32, (2 * Q_BLOCK, Q_BLOCK), 0)
    qi = lax.broadcasted_iota(jnp.int32, (2 * Q_BLOCK, Q_BLOCK), 1)
    band = (kj > qi) & (kj <= qi + WINDOW)
    bias_band = jnp.where(band, 0.0, -jnp.inf).astype(F32)
    bias_first = jnp.where(band & (kj >= Q_BLOCK), 0.0, -jnp.inf).astype(F32)

    def softmax(n):
        j, hk = divmod(n, N_KV_HEADS)
        bias = jnp.where(s == 0, bias_first, bias_band) if j == 0 else bias_band
        sink = jnp.concatenate(
            [jnp.full((1, Q_BLOCK), sinks_ref[hk * GQA_GROUP + g] * LOG2E, F32)
             for g in range(GQA_GROUP)], axis=1)
        sc = jnp.concatenate(
            [sc_scr[n, :, g * Q_BLOCK:(g + 1) * Q_BLOCK] + bias for g in range(GQA_GROUP)], axis=1)
        mx = jnp.maximum(jnp.max(sc, axis=0, keepdims=True), sink)
        p = jnp.exp2(sc - mx)
        denom = jnp.sum(p, axis=0, keepdims=True) + jnp.exp2(sink - mx)
        p_scr[n] = p.astype(BF16)
        inv_scr[n] = 1.0 / denom

    def values(n):
        j, hk = divmod(n, N_KV_HEADS)
        rows = slice(j * Q_BLOCK, (j + 1) * Q_BLOCK)
        v_win = v_scr[hk * HEAD_DIM:(hk + 1) * HEAD_DIM, j * Q_BLOCK:(j + 2) * Q_BLOCK]
        pv = jnp.dot(v_win, p_scr[n], preferred_element_type=F32) * inv_scr[n]
        for m in range(2):
            pair = jnp.concatenate([pv[:, (2 * m) * Q_BLOCK:(2 * m + 1) * Q_BLOCK],
                                    pv[:, (2 * m + 1) * Q_BLOCK:(2 * m + 2) * Q_BLOCK]], axis=0)
            col = GMLP_WIDTH + (2 * hk + m) * LANES
            mix_scr[rows, col:col + LANES] = pair.T.astype(BF16)

    def gate(c):
        rows = slice(c * CHUNK, (c + 1) * CHUNK)
        u = _gelu_tanh(proj[rows, 0:GMLP_WIDTH])
        mix_scr[rows, 0:GMLP_WIDTH] = (u * (sv_scr[rows, :] + sb_ref[...])).astype(BF16)

    scores(0); softmax(0); scores(1); softmax(1)
    for c in range(ts // CHUNK):
        gate(c)
    yield
    scores(2); softmax(2); scores(3); softmax(3)
    yield
    values(0); values(1); scores(4); softmax(4); scores(5); softmax(5)
    yield
    values(2); values(3); scores(6); softmax(6); scores(7); softmax(7)
    yield
    values(4); values(5)
    yield
    values(6); values(7)

    k_last = k_scr[:, ts:ts + Q_BLOCK, :]
    v_last = v_scr[:, ts:ts + Q_BLOCK]
    k_scr[:, 0:Q_BLOCK, :] = jnp.where(is_last_in_seq, jnp.zeros_like(k_last), k_last)
    v_scr[:, 0:Q_BLOCK] = jnp.where(is_last_in_seq, jnp.zeros_like(v_last), v_last)


def _ffn_stream(mod, x_ref, mix_scr, wout_ref, g_ref, w1_ref, w2_ref, gf_ref, o_ref, hid_scr, y_scr):
    gate1, shift2, scale2, gate2 = mod[2], mod[3], mod[4], mod[5]
    mix = jnp.dot(mix_scr[...], wout_ref[...], preferred_element_type=F32)
    yield
    x = x_ref[0] + gate1 * mix
    h = ((x * _rms_scale(x)) * (g_ref[...] * (1.0 + scale2)) + shift2).astype(BF16)
    yield

    for c in range(D_FF // FF_CHUNK):
        cols = slice(c * FF_CHUNK, (c + 1) * FF_CHUNK)
        a = jnp.dot(h, w1_ref[:, cols], preferred_element_type=F32)
        a = jnp.maximum(a, 0.0)
        hid_scr[:, cols] = (a * a).astype(BF16)
        yield
    wcol = D_MODEL // FF2_COL_BLOCKS
    gf = gf_ref[...]

    def finish(rows, cols, ff):
        y = x[rows, cols] + gate2[:, cols] * ff
        y_scr[rows, cols] = y * gf[:, cols]
        sq = y * y
        return sum(sq[:, l * LANES:(l + 1) * LANES] for l in range(wcol // LANES))

    ssq = None
    for q in range(FF2_COL_BLOCKS - 1):
        cols = slice(q * wcol, (q + 1) * wcol)
        ff = jnp.dot(hid_scr[...], w2_ref[:, cols], preferred_element_type=F32)
        part = finish(slice(None), cols, ff)
        ssq = part if ssq is None else ssq + part
        yield
    cols = slice((FF2_COL_BLOCKS - 1) * wcol, FF2_COL_BLOCKS * wcol)
    half = TOKENS_PER_STEP // 2
    for r in range(2):
        rows = slice(r * half, (r + 1) * half)
        ff = jnp.dot(hid_scr[rows, :], w2_ref[:, cols], preferred_element_type=F32)
        tot = ssq[rows] + finish(rows, cols, ff)
        scale = lax.rsqrt(jnp.sum(tot, axis=-1, keepdims=True) * (1.0 / D_MODEL) + EPS)
        o_ref[0, rows, :] = y_scr[rows, :] * scale
        yield


def _block_kernel(sinks_ref, xa_ref, xb_ref, pos_ref, mod_ref, gm_ref, win_ref, ws_ref, sb_ref,
                  invf_ref, wout_hbm, gffn_ref, w1_hbm, w2_hbm, gf_ref, o_ref,
                  mix_scr, hid_scr, k_scr, v_scr, bd_scr, lhs_scr, sv_scr, sc_scr, p_scr, inv_scr, y_scr,
                  wout_ref, w1_ref, w2_ref, w_sem, *, n_tiles, tiles_per_seq):
    i = pl.program_id(0)
    tile_a = jnp.minimum(i, n_tiles - 1)
    tile_b = jnp.maximum(i - 1, 0)

    def mod_rows(tile):
        return [mod_ref[k, pl.ds(tile // tiles_per_seq, 1), :] for k in range(N_MOD)]

    def ffn_stream():
        return _ffn_stream(mod_rows(tile_b), xb_ref, mix_scr, wout_ref, gffn_ref, w1_ref,
                           w2_ref, gf_ref, o_ref, hid_scr, y_scr)

    def mixer_stream():
        s_a = tile_a % tiles_per_seq
        pos_row = pos_ref[pl.ds(tile_a // tiles_per_seq, 1), :]
        return _mixer_stream(s_a, s_a == tiles_per_seq - 1, mod_rows(tile_a), sinks_ref,
                             xa_ref, pos_row, gm_ref, win_ref, ws_ref, sb_ref, invf_ref, mix_scr,
                             k_scr, v_scr, bd_scr, lhs_scr, sv_scr, sc_scr, p_scr, inv_scr)

    def weight_copies():
        pairs = ((wout_hbm, wout_ref), (w1_hbm, w1_ref), (w2_hbm, w2_ref))
        return [pltpu.make_async_copy(src, dst, w_sem.at[k]) for k, (src, dst) in enumerate(pairs)]

    @pl.when(i == 1)
    def _():
        for copy in weight_copies():
            copy.wait()

    @pl.when(i == 0)
    def _():
        for copy in weight_copies():
            copy.start()
        k_scr[:, 0:Q_BLOCK, :] = jnp.zeros((N_KV_HEADS, Q_BLOCK, LANES), BF16)
        v_scr[:, 0:Q_BLOCK] = jnp.zeros((KV_WIDTH, Q_BLOCK), BF16)
        for _ in mixer_stream():
            pass

    @pl.when(i == n_tiles)
    def _():
        for _ in ffn_stream():
            pass

    @pl.when(jnp.logical_and(i > 0, i < n_tiles))
    def _():
        ffn = ffn_stream()
        mixer = mixer_stream()
        order = ("ab"
                 "ab"
                 "baabaabaa"
                 "ababababab"
                 "ab")
        streams = {"a": mixer, "b": ffn}
        for name in order:
            next(streams[name], None)
        for stream in (mixer, ffn):
            for _ in stream:
                pass


def _inv_freq_column():
    inv_freq = ROPE_THETA ** (-np.arange(0, ROT_DIM, 2, dtype=np.float32) / ROT_DIM)
    return inv_freq.astype(np.float32)[:, None]


def _const_spec(shape):
    zeros = (0,) * len(shape)
    return pl.BlockSpec(shape, lambda *_: zeros, pipeline_mode=pl.Buffered(1))


def kernel(x, c, positions, w_ada, b_ada, g_mix, w_in, w_spatial, b_spatial, sinks, w_out, g_ffn,
           w_ff1, w_ff2, g_final):
    B, S, D = x.shape
    ts = TOKENS_PER_STEP
    assert D == D_MODEL and S % ts == 0 and w_ada.shape[0] == 1

    mod = pl.pallas_call(
        _ada_kernel,
        grid=(N_MOD,),
        in_specs=[pl.BlockSpec((B, D), lambda i: (0, 0)),
                  pl.BlockSpec((D, D), lambda i: (0, i)),
                  pl.BlockSpec((1, D), lambda i: (0, i))],
        out_specs=pl.BlockSpec((1, B, D), lambda i: (i, 0, 0)),
        out_shape=jax.ShapeDtypeStruct((N_MOD, B, D), F32),
        compiler_params=pltpu.CompilerParams(dimension_semantics=("arbitrary",),
                                             vmem_limit_bytes=VMEM_LIMIT_BYTES),
        name="ada_mod",
    )(c, w_ada[0], b_ada)

    sbias = jnp.repeat(b_spatial[0].T, HEAD_DIM, axis=1)

    tiles_per_seq = S // ts
    n_tiles = B * tiles_per_seq

    def tile_a(i):
        return jnp.minimum(i, n_tiles - 1)

    def tile_b(i):
        return jnp.maximum(i - 1, 0)

    def tok_spec(tile):
        return pl.BlockSpec((1, ts, D), lambda i: (tile(i) // tiles_per_seq, tile(i) % tiles_per_seq, 0))

    return pl.pallas_call(
        functools.partial(_block_kernel, n_tiles=n_tiles, tiles_per_seq=tiles_per_seq),
        grid=(n_tiles + 1,),
        in_specs=[pl.BlockSpec(memory_space=pltpu.SMEM),
                  tok_spec(tile_a),
                  tok_spec(tile_b),
                  pl.BlockSpec((B, ts), lambda i: (0, tile_a(i) % tiles_per_seq)),
                  _const_spec((N_MOD, B, D)),
                  _const_spec((1, D)),
                  _const_spec((D, IN_PROJ_WIDTH)),
                  _const_spec((GMLP_GROUPS, CHUNK, CHUNK)),
                  _const_spec((CHUNK, GMLP_WIDTH)),
                  _const_spec((ROT_DIM // 2, 1)),
                  pl.BlockSpec(memory_space=pl.ANY),
                  _const_spec((1, D)),
                  pl.BlockSpec(memory_space=pl.ANY),
                  pl.BlockSpec(memory_space=pl.ANY),
                  _const_spec((1, D))],
        out_specs=tok_spec(tile_b),
        out_shape=jax.ShapeDtypeStruct((B, S, D), F32),
        scratch_shapes=[pltpu.VMEM((ts, D), BF16),
                        pltpu.VMEM((ts, D_FF), BF16),
                        pltpu.VMEM((N_KV_HEADS, Q_BLOCK + ts, LANES), BF16),
                        pltpu.VMEM((KV_WIDTH, Q_BLOCK + ts), BF16),
                        pltpu.VMEM((N_SUB * N_PAIRS, 2 * CHUNK, LANES), BF16),
                        pltpu.VMEM((N_ATT, GQA_GROUP * Q_BLOCK, LANES), BF16),
                        pltpu.VMEM((ts, GMLP_WIDTH), F32),
                        pltpu.VMEM((N_ATT, 2 * Q_BLOCK, GQA_GROUP * Q_BLOCK), F32),
                        pltpu.VMEM((N_ATT, 2 * Q_BLOCK, GQA_GROUP * Q_BLOCK), BF16),
                        pltpu.VMEM((N_ATT, 1, GQA_GROUP * Q_BLOCK), F32),
                        pltpu.VMEM((ts, D), F32),
                        pltpu.VMEM((D, D), BF16),
                        pltpu.VMEM((D, D_FF), BF16),
                        pltpu.VMEM((D_FF, D), BF16),
                        pltpu.SemaphoreType.DMA((3,))],
        compiler_params=pltpu.CompilerParams(dimension_semantics=("arbitrary",),
                                             vmem_limit_bytes=VMEM_LIMIT_BYTES),
        name="decoder_block",
    )(sinks[0], x, x, positions, mod, g_mix, w_in[0].astype(BF16), w_spatial[0], sbias,
      jnp.asarray(_inv_freq_column()), w_out[0].astype(BF16), g_ffn,
      w_ff1[0].astype(BF16), w_ff2[0].astype(BF16), g_final[None])
```

```python
import functools
import math

import numpy as np
import jax
import jax.numpy as jnp
from jax import lax
from jax.experimental import pallas as pl
from jax.experimental.pallas import tpu as pltpu

D_MODEL = 1024
HEAD_DIM = 64
GMLP_GROUPS = 8
GMLP_WIDTH = GMLP_GROUPS * HEAD_DIM
CHUNK = 128
N_Q_HEADS = 8
N_KV_HEADS = 2
GQA_GROUP = N_Q_HEADS // N_KV_HEADS
ATTN_WIDTH = N_Q_HEADS * HEAD_DIM
KV_WIDTH = N_KV_HEADS * HEAD_DIM
WINDOW = 128
Q_BLOCK = 128
ROPE_THETA = 500000.0
ROT_DIM = HEAD_DIM // 4
D_FF = 4 * D_MODEL
IN_PROJ_WIDTH = 2 * GMLP_WIDTH + ATTN_WIDTH + 2 * KV_WIDTH
N_MOD = 6
EPS = 1e-5
LOG2E = math.log2(math.e)

LANES = 128
SUBLANES = 8
TOKENS_PER_STEP = 512
FF_CHUNK = 1024
FF2_COL_BLOCKS = 4
ADA_ROWS = 512
VMEM_LIMIT_BYTES = 60 * 1024 * 1024

N_SUB = TOKENS_PER_STEP // Q_BLOCK
N_ATT = N_SUB * N_KV_HEADS
N_PAIRS = GMLP_GROUPS // 2

F32 = jnp.float32
BF16 = jnp.bfloat16

assert ROT_DIM // 2 == SUBLANES and LANES == 2 * HEAD_DIM
assert CHUNK == LANES and Q_BLOCK == LANES and WINDOW == Q_BLOCK
assert TOKENS_PER_STEP % Q_BLOCK == 0 and D_FF % FF_CHUNK == 0 and D_MODEL % ADA_ROWS == 0


def _ada_kernel(c_ref, w_ref, b_ref, o_ref):
    k = pl.program_id(0)
    c = c_ref[...]
    c_act = c * (1.0 / (1.0 + jnp.exp(-c)))
    part = jnp.dot(c_act, w_ref[...], preferred_element_type=F32)
    for m in range(N_MOD):
        cols = slice(m * D_MODEL, (m + 1) * D_MODEL)

        @pl.when(k == 0)
        def _():
            o_ref[m] = part[:, cols] + b_ref[:, cols]

        @pl.when(k > 0)
        def _():
            o_ref[m] += part[:, cols]


def _rms_scale(x):
    return lax.rsqrt(jnp.mean(x * x, axis=-1, keepdims=True) + EPS)


def _gelu_tanh(x):
    c = 2.0 * math.sqrt(2.0 / math.pi) * LOG2E
    return x / (1.0 + jnp.exp2(x * (-c - (c * 0.044715) * (x * x))))


def _mixer_stream(s, is_last_in_seq, mod, sinks_ref, x_ref, pos_row, g_ref, win_ref, ws_ref, sb_ref,
                  invf_ref, mix_scr, k_scr, v_scr, bd_scr, lhs_scr, sv_scr, sc_scr, p_scr, inv_scr):
    ts = TOKENS_PER_STEP
    x = x_ref[0]
    shift1, scale1 = mod[0], mod[1]

    h = ((x * _rms_scale(x)) * (g_ref[...] * (1.0 + scale1)) + shift1).astype(BF16)
    yield
    proj = jnp.dot(h, win_ref[...], preferred_element_type=F32)
    yield

    lane = lax.broadcasted_iota(jnp.int32, (1, LANES), 1)
    left = lane < HEAD_DIM

    t_idx = lax.broadcasted_iota(jnp.int32, (CHUNK, 2 * CHUNK), 0)
    s_idx = lax.broadcasted_iota(jnp.int32, (CHUNK, 2 * CHUNK), 1)
    causal = (s_idx & (CHUNK - 1)) <= t_idx
    w_pairs = [jnp.where(causal, jnp.concatenate([ws_ref[2 * p], ws_ref[2 * p + 1]], axis=1), 0.0).astype(BF16)
               for p in range(N_PAIRS)]
    for c in range(ts // CHUNK):
        rows = slice(c * CHUNK, (c + 1) * CHUNK)
        v = _gelu_tanh(proj[rows, GMLP_WIDTH:2 * GMLP_WIDTH])
        for p in range(N_PAIRS):
            vp = v[:, p * LANES:(p + 1) * LANES]
            bd_scr[c * N_PAIRS + p, 0:CHUNK, :] = jnp.where(left, vp, 0.0).astype(BF16)
            bd_scr[c * N_PAIRS + p, CHUNK:2 * CHUNK, :] = jnp.where(left, 0.0, vp).astype(BF16)
        yield

    ang = invf_ref[...] * pos_row.astype(F32)
    cos8 = jnp.cos(ang)
    sin8 = jnp.sin(ang)
    one8 = jnp.ones_like(ang)
    zero8 = jnp.zeros_like(ang)
    tiles_per_head = HEAD_DIM // SUBLANES
    heads_per_vreg = LANES // HEAD_DIM
    cos_t = jnp.concatenate(([cos8, cos8] + [one8] * (tiles_per_head - 2)) * heads_per_vreg, axis=0).T
    sin_a = jnp.concatenate(([-sin8] + [zero8] * (tiles_per_head - 1)) * heads_per_vreg, axis=0).T
    sin_b = jnp.concatenate(([zero8, sin8] + [zero8] * (tiles_per_head - 2)) * heads_per_vreg, axis=0).T

    def rope(t):
        return (t * cos_t + pltpu.roll(t, LANES - ROT_DIM // 2, 1) * sin_a
                + pltpu.roll(t, ROT_DIM // 2, 1) * sin_b)

    o = 2 * GMLP_WIDTH
    scale = LOG2E / math.sqrt(HEAD_DIM)
    zero = jnp.zeros((), BF16)
    for i in range(ATTN_WIDTH // LANES):
        qp = (rope(proj[:, o + i * LANES:o + (i + 1) * LANES]) * scale).astype(BF16)
        hk, m = divmod(i, 2)
        halves = (jnp.where(left, qp, zero), jnp.where(left, zero, qp))
        for j in range(N_SUB):
            for e in range(2):
                g = 2 * m + e
                lhs_scr[j * N_KV_HEADS + hk, g * Q_BLOCK:(g + 1) * Q_BLOCK, :] = (
                    halves[e][j * Q_BLOCK:(j + 1) * Q_BLOCK])
    yield
    o += ATTN_WIDTH
    k = rope(proj[:, o:o + KV_WIDTH])
    v_att = proj[:, o + KV_WIDTH:o + 2 * KV_WIDTH]
    k_sw = pltpu.roll(k, HEAD_DIM, 1)
    k_scr[0, Q_BLOCK:Q_BLOCK + ts, :] = jnp.where(left, k, k_sw).astype(BF16)
    k_scr[1, Q_BLOCK:Q_BLOCK + ts, :] = jnp.where(left, k_sw, k).astype(BF16)
    v_scr[:, Q_BLOCK:Q_BLOCK + ts] = v_att.T.astype(BF16)
    yield

    for c in range(ts // CHUNK):
        rows = slice(c * CHUNK, (c + 1) * CHUNK)
        for p in range(N_PAIRS):
            sv_scr[rows, p * LANES:(p + 1) * LANES] = jnp.dot(
                w_pairs[p], bd_scr[c * N_PAIRS + p], preferred_element_type=F32)

    def scores(n):
        j, hk = divmod(n, N_KV_HEADS)
        k_win = k_scr[hk, j * Q_BLOCK:(j + 2) * Q_BLOCK, :]
        sc_scr[n] = lax.dot_general(k_win, lhs_scr[n], (((1,), (1,)), ((), ())),
                                    preferred_element_type=F32)

    kj = lax.broadcasted_iota(jnp.int32, (2 * Q_BLOCK, Q_BLOCK), 0)
    qi = lax.broadcasted_iota(jnp.int32, (2 * Q_BLOCK, Q_BLOCK), 1)
    band = (kj > qi) & (kj <= qi + WINDOW)
    bias_band = jnp.where(band, 0.0, -jnp.inf).astype(F32)
    bias_first = jnp.where(band & (kj >= Q_BLOCK), 0.0, -jnp.inf).astype(F32)

    def softmax(n):
        j, hk = divmod(n, N_KV_HEADS)
        bias = jnp.where(s == 0, bias_first, bias_band) if j == 0 else bias_band
        sink = jnp.concatenate(
            [jnp.full((1, Q_BLOCK), sinks_ref[hk * GQA_GROUP + g] * LOG2E, F32)
             for g in range(GQA_GROUP)], axis=1)
        sc = jnp.concatenate(
            [sc_scr[n, :, g * Q_BLOCK:(g + 1) * Q_BLOCK] + bias for g in range(GQA_GROUP)], axis=1)
        mx = jnp.maximum(jnp.max(sc, axis=0, keepdims=True), sink)
        p = jnp.exp2(sc - mx)
        denom = jnp.sum(p, axis=0, keepdims=True) + jnp.exp2(sink - mx)
        p_scr[n] = p.astype(BF16)
        inv_scr[n] = 1.0 / denom

    def values(n):
        j, hk = divmod(n, N_KV_HEADS)
        rows = slice(j * Q_BLOCK, (j + 1) * Q_BLOCK)
        v_win = v_scr[hk * HEAD_DIM:(hk + 1) * HEAD_DIM, j * Q_BLOCK:(j + 2) * Q_BLOCK]
        pv = jnp.dot(v_win, p_scr[n], preferred_element_type=F32) * inv_scr[n]
        for m in range(2):
            pair = jnp.concatenate([pv[:, (2 * m) * Q_BLOCK:(2 * m + 1) * Q_BLOCK],
                                    pv[:, (2 * m + 1) * Q_BLOCK:(2 * m + 2) * Q_BLOCK]], axis=0)
            col = GMLP_WIDTH + (2 * hk + m) * LANES
            mix_scr[rows, col:col + LANES] = pair.T.astype(BF16)

    def gate(c):
        rows = slice(c * CHUNK, (c + 1) * CHUNK)
        u = _gelu_tanh(proj[rows, 0:GMLP_WIDTH])
        mix_scr[rows, 0:GMLP_WIDTH] = (u * (sv_scr[rows, :] + sb_ref[...])).astype(BF16)

    scores(0); softmax(0); scores(1); softmax(1)
    for c in range(ts // CHUNK):
        gate(c)
    yield
    scores(2); softmax(2); scores(3); softmax(3)
    yield
    values(0); values(1); scores(4); softmax(4); scores(5); softmax(5)
    yield
    values(2); values(3); scores(6); softmax(6); scores(7); softmax(7)
    yield
    values(4); values(5)
    yield
    values(6); values(7)

    k_last = k_scr[:, ts:ts + Q_BLOCK, :]
    v_last = v_scr[:, ts:ts + Q_BLOCK]
    k_scr[:, 0:Q_BLOCK, :] = jnp.where(is_last_in_seq, jnp.zeros_like(k_last), k_last)
    v_scr[:, 0:Q_BLOCK] = jnp.where(is_last_in_seq, jnp.zeros_like(v_last), v_last)


def _ffn_stream(mod, x_ref, mix_scr, wout_ref, g_ref, w1_ref, w2_ref, gf_ref, o_ref, hid_scr, y_scr):
    gate1, shift2, scale2, gate2 = mod[2], mod[3], mod[4], mod[5]
    mix = jnp.dot(mix_scr[...], wout_ref[...], preferred_element_type=F32)
    yield
    x = x_ref[0] + gate1 * mix
    h = ((x * _rms_scale(x)) * (g_ref[...] * (1.0 + scale2)) + shift2).astype(BF16)
    yield

    for c in range(D_FF // FF_CHUNK):
        cols = slice(c * FF_CHUNK, (c + 1) * FF_CHUNK)
        a = jnp.dot(h, w1_ref[:, cols], preferred_element_type=F32)
        a = jnp.maximum(a, 0.0)
        hid_scr[:, cols] = (a * a).astype(BF16)
        yield
    wcol = D_MODEL // FF2_COL_BLOCKS
    gf = gf_ref[...]

    def finish(rows, cols, ff):
        y = x[rows, cols] + gate2[:, cols] * ff
        y_scr[rows, cols] = y * gf[:, cols]
        sq = y * y
        return sum(sq[:, l * LANES:(l + 1) * LANES] for l in range(wcol // LANES))

    ssq = None
    for q in range(FF2_COL_BLOCKS - 1):
        cols = slice(q * wcol, (q + 1) * wcol)
        ff = jnp.dot(hid_scr[...], w2_ref[:, cols], preferred_element_type=F32)
        part = finish(slice(None), cols, ff)
        ssq = part if ssq is None else ssq + part
        yield
    cols = slice((FF2_COL_BLOCKS - 1) * wcol, FF2_COL_BLOCKS * wcol)
    half = TOKENS_PER_STEP // 2
    for r in range(2):
        rows = slice(r * half, (r + 1) * half)
        ff = jnp.dot(hid_scr[rows, :], w2_ref[:, cols], preferred_element_type=F32)
        tot = ssq[rows] + finish(rows, cols, ff)
        scale = lax.rsqrt(jnp.sum(tot, axis=-1, keepdims=True) * (1.0 / D_MODEL) + EPS)
        o_ref[0, rows, :] = y_scr[rows, :] * scale
        yield


def _block_kernel(sinks_ref, xa_ref, xb_ref, pos_ref, mod_ref, gm_ref, win_ref, ws_ref, sb_ref,
                  invf_ref, wout_hbm, gffn_ref, w1_hbm, w2_hbm, gf_ref, o_ref,
                  mix_scr, hid_scr, k_scr, v_scr, bd_scr, lhs_scr, sv_scr, sc_scr, p_scr, inv_scr, y_scr,
                  wout_ref, w1_ref, w2_ref, w_sem, *, n_tiles, tiles_per_seq):
    i = pl.program_id(0)
    tile_a = jnp.minimum(i, n_tiles - 1)
    tile_b = jnp.maximum(i - 1, 0)

    def mod_rows(tile):
        return [mod_ref[k, pl.ds(tile // tiles_per_seq, 1), :] for k in range(N_MOD)]

    def ffn_stream():
        return _ffn_stream(mod_rows(tile_b), xb_ref, mix_scr, wout_ref, gffn_ref, w1_ref,
                           w2_ref, gf_ref, o_ref, hid_scr, y_scr)

    def mixer_stream():
        s_a = tile_a % tiles_per_seq
        pos_row = pos_ref[pl.ds(tile_a // tiles_per_seq, 1), :]
        return _mixer_stream(s_a, s_a == tiles_per_seq - 1, mod_rows(tile_a), sinks_ref,
                             xa_ref, pos_row, gm_ref, win_ref, ws_ref, sb_ref, invf_ref, mix_scr,
                             k_scr, v_scr, bd_scr, lhs_scr, sv_scr, sc_scr, p_scr, inv_scr)

    def weight_copies():
        pairs = ((wout_hbm, wout_ref), (w1_hbm, w1_ref), (w2_hbm, w2_ref))
        return [pltpu.make_async_copy(src, dst, w_sem.at[k]) for k, (src, dst) in enumerate(pairs)]

    @pl.when(i == 1)
    def _():
        for copy in weight_copies():
            copy.wait()

    @pl.when(i == 0)
    def _():
        for copy in weight_copies():
            copy.start()
        k_scr[:, 0:Q_BLOCK, :] = jnp.zeros((N_KV_HEADS, Q_BLOCK, LANES), BF16)
        v_scr[:, 0:Q_BLOCK] = jnp.zeros((KV_WIDTH, Q_BLOCK), BF16)
        for _ in mixer_stream():
            pass

    @pl.when(i == n_tiles)
    def _():
        for _ in ffn_stream():
            pass

    @pl.when(jnp.logical_and(i > 0, i < n_tiles))
    def _():
        ffn = ffn_stream()
        mixer = mixer_stream()
        order = ("ab"
                 "ab"
                 "baabaabaa"
                 "ababababab"
                 "ab")
        streams = {"a": mixer, "b": ffn}
        for name in order:
            next(streams[name], None)
        for stream in (mixer, ffn):
            for _ in stream:
                pass


def _inv_freq_column():
    inv_freq = ROPE_THETA ** (-np.arange(0, ROT_DIM, 2, dtype=np.float32) / ROT_DIM)
    return inv_freq.astype(np.float32)[:, None]


def _const_spec(shape):
    zeros = (0,) * len(shape)
    return pl.BlockSpec(shape, lambda *_: zeros, pipeline_mode=pl.Buffered(1))


def kernel(x, c, positions, w_ada, b_ada, g_mix, w_in, w_spatial, b_spatial, sinks, w_out, g_ffn,
           w_ff1, w_ff2, g_final):
    B, S, D = x.shape
    ts = TOKENS_PER_STEP
    assert D == D_MODEL and S % ts == 0 and w_ada.shape[0] == 1

    mod = pl.pallas_call(
        _ada_kernel,
        grid=(D // ADA_ROWS,),
        in_specs=[pl.BlockSpec((B, ADA_ROWS), lambda k: (0, k)),
                  pl.BlockSpec((ADA_ROWS, N_MOD * D), lambda k: (k, 0)),
                  pl.BlockSpec((1, N_MOD * D), lambda k: (0, 0))],
        out_specs=pl.BlockSpec((N_MOD, B, D), lambda k: (0, 0, 0)),
        out_shape=jax.ShapeDtypeStruct((N_MOD, B, D), F32),
        compiler_params=pltpu.CompilerParams(dimension_semantics=("arbitrary",),
                                             vmem_limit_bytes=VMEM_LIMIT_BYTES),
        name="ada_mod",
    )(c, w_ada[0], b_ada)

    sbias = jnp.repeat(b_spatial[0].T, HEAD_DIM, axis=1)

    tiles_per_seq = S // ts
    n_tiles = B * tiles_per_seq

    def tile_a(i):
        return jnp.minimum(i, n_tiles - 1)

    def tile_b(i):
        return jnp.maximum(i - 1, 0)

    def tok_spec(tile):
        return pl.BlockSpec((1, ts, D), lambda i: (tile(i) // tiles_per_seq, tile(i) % tiles_per_seq, 0))

    return pl.pallas_call(
        functools.partial(_block_kernel, n_tiles=n_tiles, tiles_per_seq=tiles_per_seq),
        grid=(n_tiles + 1,),
        in_specs=[pl.BlockSpec(memory_space=pltpu.SMEM),
                  tok_spec(tile_a),
                  tok_spec(tile_b),
                  pl.BlockSpec((B, ts), lambda i: (0, tile_a(i) % tiles_per_seq)),
                  _const_spec((N_MOD, B, D)),
                  _const_spec((1, D)),
                  _const_spec((D, IN_PROJ_WIDTH)),
                  _const_spec((GMLP_GROUPS, CHUNK, CHUNK)),
                  _const_spec((CHUNK, GMLP_WIDTH)),
                  _const_spec((ROT_DIM // 2, 1)),
                  pl.BlockSpec(memory_space=pl.ANY),
                  _const_spec((1, D)),
                  pl.BlockSpec(memory_space=pl.ANY),
                  pl.BlockSpec(memory_space=pl.ANY),
                  _const_spec((1, D))],
        out_specs=tok_spec(tile_b),
        out_shape=jax.ShapeDtypeStruct((B, S, D), F32),
        scratch_shapes=[pltpu.VMEM((ts, D), BF16),
                        pltpu.VMEM((ts, D_FF), BF16),
                        pltpu.VMEM((N_KV_HEADS, Q_BLOCK + ts, LANES), BF16),
                        pltpu.VMEM((KV_WIDTH, Q_BLOCK + ts), BF16),
                        pltpu.VMEM((N_SUB * N_PAIRS, 2 * CHUNK, LANES), BF16),
                        pltpu.VMEM((N_ATT, GQA_GROUP * Q_BLOCK, LANES), BF16),
                        pltpu.VMEM((ts, GMLP_WIDTH), F32),
                        pltpu.VMEM((N_ATT, 2 * Q_BLOCK, GQA_GROUP * Q_BLOCK), F32),
                        pltpu.VMEM((N_ATT, 2 * Q_BLOCK, GQA_GROUP * Q_BLOCK), BF16),
                        pltpu.VMEM((N_ATT, 1, GQA_GROUP * Q_BLOCK), F32),
                        pltpu.VMEM((ts, D), F32),
                        pltpu.VMEM((D, D), BF16),
                        pltpu.VMEM((D, D_FF), BF16),
                        pltpu.VMEM((D_FF, D), BF16),
                        pltpu.SemaphoreType.DMA((3,))],
        compiler_params=pltpu.CompilerParams(dimension_semantics=("arbitrary",),
                                             vmem_limit_bytes=VMEM_LIMIT_BYTES),
        name="decoder_block",
    )(sinks[0], x, x, positions, mod, g_mix, w_in[0].astype(BF16), w_spatial[0], sbias,
      jnp.asarray(_inv_freq_column()), w_out[0].astype(BF16), g_ffn,
      w_ff1[0].astype(BF16), w_ff2[0].astype(BF16), g_final[None])
```

```python
import functools
import math

import numpy as np
import jax
import jax.numpy as jnp
from jax import lax
from jax.experimental import pallas as pl
from jax.experimental.pallas import tpu as pltpu

D_MODEL = 1024
HEAD_DIM = 64
GMLP_GROUPS = 8
GMLP_WIDTH = GMLP_GROUPS * HEAD_DIM
CHUNK = 128
N_Q_HEADS = 8
N_KV_HEADS = 2
GQA_GROUP = N_Q_HEADS // N_KV_HEADS
ATTN_WIDTH = N_Q_HEADS * HEAD_DIM
KV_WIDTH = N_KV_HEADS * HEAD_DIM
WINDOW = 128
Q_BLOCK = 128
ROPE_THETA = 500000.0
ROT_DIM = HEAD_DIM // 4
D_FF = 4 * D_MODEL
IN_PROJ_WIDTH = 2 * GMLP_WIDTH + ATTN_WIDTH + 2 * KV_WIDTH
N_MOD = 6
EPS = 1e-5
LOG2E = math.log2(math.e)

LANES = 128
SUBLANES = 8
TOKENS_PER_STEP = 512
FF_CHUNK = 1024
FF2_COL_BLOCKS = 4
ADA_ROWS = 512
VMEM_LIMIT_BYTES = 60 * 1024 * 1024

N_SUB = TOKENS_PER_STEP // Q_BLOCK
N_ATT = N_SUB * N_KV_HEADS
N_PAIRS = GMLP_GROUPS // 2

F32 = jnp.float32
BF16 = jnp.bfloat16

assert ROT_DIM // 2 == SUBLANES and LANES == 2 * HEAD_DIM
assert CHUNK == LANES and Q_BLOCK == LANES and WINDOW == Q_BLOCK
assert TOKENS_PER_STEP % Q_BLOCK == 0 and D_FF % FF_CHUNK == 0 and D_MODEL % ADA_ROWS == 0


def _ada_kernel(c_ref, w_ref, b_ref, o_ref):
    k = pl.program_id(0)
    c = c_ref[...]
    c_act = c * (1.0 / (1.0 + jnp.exp(-c)))
    part = jnp.dot(c_act, w_ref[...], preferred_element_type=F32)
    for m in range(N_MOD):
        cols = slice(m * D_MODEL, (m + 1) * D_MODEL)

        @pl.when(k == 0)
        def _():
            o_ref[m] = part[:, cols] + b_ref[:, cols]

        @pl.when(k > 0)
        def _():
            o_ref[m] += part[:, cols]


def _rms_scale(x):
    return lax.rsqrt(jnp.mean(x * x, axis=-1, keepdims=True) + EPS)


def _gelu_tanh(x):
    c = 2.0 * math.sqrt(2.0 / math.pi) * LOG2E
    return x / (1.0 + jnp.exp2(x * (-c - (c * 0.044715) * (x * x))))


def _mixer_stream(s, is_last_in_seq, mod, sinks_ref, x_ref, pos_row, g_ref, win_ref, ws_ref, sb_ref,
                  invf_ref, mix_scr, k_scr, v_scr, bd_scr, lhs_scr, sv_scr, sc_scr, p_scr, inv_scr):
    ts = TOKENS_PER_STEP
    x = x_ref[0]
    shift1, scale1 = mod[0], mod[1]

    h = ((x * _rms_scale(x)) * (g_ref[...] * (1.0 + scale1)) + shift1).astype(BF16)
    yield
    proj = jnp.dot(h, win_ref[...], preferred_element_type=F32)
    yield

    lane = lax.broadcasted_iota(jnp.int32, (1, LANES), 1)
    left = lane < HEAD_DIM

    t_idx = lax.broadcasted_iota(jnp.int32, (CHUNK, 2 * CHUNK), 0)
    s_idx = lax.broadcasted_iota(jnp.int32, (CHUNK, 2 * CHUNK), 1)
    causal = (s_idx & (CHUNK - 1)) <= t_idx
    w_pairs = [jnp.where(causal, jnp.concatenate([ws_ref[2 * p], ws_ref[2 * p + 1]], axis=1), 0.0).astype(BF16)
               for p in range(N_PAIRS)]
    for c in range(ts // CHUNK):
        rows = slice(c * CHUNK, (c + 1) * CHUNK)
        v = _gelu_tanh(proj[rows, GMLP_WIDTH:2 * GMLP_WIDTH])
        for p in range(N_PAIRS):
            vp = v[:, p * LANES:(p + 1) * LANES]
            bd_scr[c * N_PAIRS + p, 0:CHUNK, :] = jnp.where(left, vp, 0.0).astype(BF16)
            bd_scr[c * N_PAIRS + p, CHUNK:2 * CHUNK, :] = jnp.where(left, 0.0, vp).astype(BF16)
        yield

    ang = invf_ref[...] * pos_row.astype(F32)
    cos8 = jnp.cos(ang)
    sin8 = jnp.sin(ang)
    one8 = jnp.ones_like(ang)
    zero8 = jnp.zeros_like(ang)
    tiles_per_head = HEAD_DIM // SUBLANES
    heads_per_vreg = LANES // HEAD_DIM
    cos_t = jnp.concatenate(([cos8, cos8] + [one8] * (tiles_per_head - 2)) * heads_per_vreg, axis=0).T
    sin_a = jnp.concatenate(([-sin8] + [zero8] * (tiles_per_head - 1)) * heads_per_vreg, axis=0).T
    sin_b = jnp.concatenate(([zero8, sin8] + [zero8] * (tiles_per_head - 2)) * heads_per_vreg, axis=0).T

    def rope(t):
        return (t * cos_t + pltpu.roll(t, LANES - ROT_DIM // 2, 1) * sin_a
                + pltpu.roll(t, ROT_DIM // 2, 1) * sin_b)

    o = 2 * GMLP_WIDTH
    scale = LOG2E / math.sqrt(HEAD_DIM)
    zero = jnp.zeros((), BF16)
    for i in range(ATTN_WIDTH // LANES):
        qp = (rope(proj[:, o + i * LANES:o + (i + 1) * LANES]) * scale).astype(BF16)
        hk, m = divmod(i, 2)
        halves = (jnp.where(left, qp, zero), jnp.where(left, zero, qp))
        for j in range(N_SUB):
            for e in range(2):
                g = 2 * m + e
                lhs_scr[j * N_KV_HEADS + hk, g * Q_BLOCK:(g + 1) * Q_BLOCK, :] = (
                    halves[e][j * Q_BLOCK:(j + 1) * Q_BLOCK])
    yield
    o += ATTN_WIDTH
    k = rope(proj[:, o:o + KV_WIDTH])
    v_att = proj[:, o + KV_WIDTH:o + 2 * KV_WIDTH]
    k_sw = pltpu.roll(k, HEAD_DIM, 1)
    k_scr[0, Q_BLOCK:Q_BLOCK + ts, :] = jnp.where(left, k, k_sw).astype(BF16)
    k_scr[1, Q_BLOCK:Q_BLOCK + ts, :] = jnp.where(left, k_sw, k).astype(BF16)
    v_scr[:, Q_BLOCK:Q_BLOCK + ts] = v_att.T.astype(BF16)
    yield

    for c in range(ts // CHUNK):
        rows = slice(c * CHUNK, (c + 1) * CHUNK)
        for p in range(N_PAIRS):
            sv_scr[rows, p * LANES:(p + 1) * LANES] = jnp.dot(
                w_pairs[p], bd_scr[c * N_PAIRS + p], preferred_element_type=F32)

    def scores(n):
        j, hk = divmod(n, N_KV_HEADS)
        k_win = k_scr[hk, j * Q_BLOCK:(j + 2) * Q_BLOCK, :]
        sc_scr[n] = lax.dot_general(k_win, lhs_scr[n], (((1,), (1,)), ((), ())),
                                    preferred_element_type=F32)

    kj = lax.broadcasted_iota(jnp.int32, (2 * Q_BLOCK, Q_BLOCK), 0)
    qi = lax.broadcasted_iota(jnp.int32, (2 * Q_BLOCK, Q_BLOCK), 1)
    band = (kj > qi) & (kj <= qi + WINDOW)
    bias_band = jnp.where(band, 0.0, -jnp.inf).astype(F32)
    bias_first = jnp.where(band & (kj >= Q_BLOCK), 0.0, -jnp.inf).astype(F32)

    def softmax(n):
        j, hk = divmod(n, N_KV_HEADS)
        bias = jnp.where(s == 0, bias_first, bias_band) if j == 0 else bias_band
        sink = jnp.concatenate(
            [jnp.full((1, Q_BLOCK), sinks_ref[hk * GQA_GROUP + g] * LOG2E, F32)
             for g in range(GQA_GROUP)], axis=1)
        sc = jnp.concatenate(
            [sc_scr[n, :, g * Q_BLOCK:(g + 1) * Q_BLOCK] + bias for g in range(GQA_GROUP)], axis=1)
        mx = jnp.maximum(jnp.max(sc, axis=0, keepdims=True), sink)
        p = jnp.exp2(sc - mx)
        denom = jnp.sum(p, axis=0, keepdims=True) + jnp.exp2(sink - mx)
        p_scr[n] = p.astype(BF16)
        inv_scr[n] = 1.0 / denom

    def values(n):
        j, hk = divmod(n, N_KV_HEADS)
        rows = slice(j * Q_BLOCK, (j + 1) * Q_BLOCK)
        v_win = v_scr[hk * HEAD_DIM:(hk + 1) * HEAD_DIM, j * Q_BLOCK:(j + 2) * Q_BLOCK]
        pv = jnp.dot(v_win, p_scr[n], preferred_element_type=F32) * inv_scr[n]
        for m in range(2):
            pair = jnp.concatenate([pv[:, (2 * m) * Q_BLOCK:(2 * m + 1) * Q_BLOCK],
                                    pv[:, (2 * m + 1) * Q_BLOCK:(2 * m + 2) * Q_BLOCK]], axis=0)
            col = GMLP_WIDTH + (2 * hk + m) * LANES
            mix_scr[rows, col:col + LANES] = pair.T.astype(BF16)

    def gate(c):
        rows = slice(c * CHUNK, (c + 1) * CHUNK)
        u = _gelu_tanh(proj[rows, 0:GMLP_WIDTH])
        mix_scr[rows, 0:GMLP_WIDTH] = (u * (sv_scr[rows, :] + sb_ref[...])).astype(BF16)

    scores(0); softmax(0); scores(1); softmax(1)
    for c in range(ts // CHUNK):
        gate(c)
    yield
    scores(2); softmax(2); scores(3); softmax(3)
    yield
    values(0); values(1); scores(4); softmax(4); scores(5); softmax(5)
    yield
    values(2); values(3); scores(6); softmax(6); scores(7); softmax(7)
    yield
    values(4); values(5)
    yield
    values(6); values(7)

    k_last = k_scr[:, ts:ts + Q_BLOCK, :]
    v_last = v_scr[:, ts:ts + Q_BLOCK]
    k_scr[:, 0:Q_BLOCK, :] = jnp.where(is_last_in_seq, jnp.zeros_like(k_last), k_last)
    v_scr[:, 0:Q_BLOCK] = jnp.where(is_last_in_seq, jnp.zeros_like(v_last), v_last)


def _ffn_stream(mod, x_ref, mix_scr, wout_ref, g_ref, w1_ref, w2_ref, gf_ref, o_ref, hid_scr, y_scr):
    gate1, shift2, scale2, gate2 = mod[2], mod[3], mod[4], mod[5]
    mix = jnp.dot(mix_scr[...], wout_ref[...], preferred_element_type=F32)
    yield
    x = x_ref[0] + gate1 * mix
    h = ((x * _rms_scale(x)) * (g_ref[...] * (1.0 + scale2)) + shift2).astype(BF16)
    yield

    for c in range(D_FF // FF_CHUNK):
        cols = slice(c * FF_CHUNK, (c + 1) * FF_CHUNK)
        a = jnp.dot(h, w1_ref[:, cols], preferred_element_type=F32)
        a = jnp.maximum(a, 0.0)
        hid_scr[:, cols] = (a * a).astype(BF16)
        yield
    wcol = D_MODEL // FF2_COL_BLOCKS
    gf = gf_ref[...]

    def finish(rows, cols, ff):
        y = x[rows, cols] + gate2[:, cols] * ff
        y_scr[rows, cols] = y * gf[:, cols]
        sq = y * y
        return sum(sq[:, l * LANES:(l + 1) * LANES] for l in range(wcol // LANES))

    ssq = None
    for q in range(FF2_COL_BLOCKS - 1):
        cols = slice(q * wcol, (q + 1) * wcol)
        ff = jnp.dot(hid_scr[...], w2_ref[:, cols], preferred_element_type=F32)
        part = finish(slice(None), cols, ff)
        ssq = part if ssq is None else ssq + part
        yield
    cols = slice((FF2_COL_BLOCKS - 1) * wcol, FF2_COL_BLOCKS * wcol)
    half = TOKENS_PER_STEP // 2
    for r in range(2):
        rows = slice(r * half, (r + 1) * half)
        ff = jnp.dot(hid_scr[rows, :], w2_ref[:, cols], preferred_element_type=F32)
        tot = ssq[rows] + finish(rows, cols, ff)
        scale = lax.rsqrt(jnp.sum(tot, axis=-1, keepdims=True) * (1.0 / D_MODEL) + EPS)
        o_ref[0, rows, :] = y_scr[rows, :] * scale
        yield


def _block_kernel(sinks_ref, xa_ref, xb_ref, pos_ref, mod_ref, gm_ref, win_ref, ws_ref, sb_ref,
                  invf_ref, wout_hbm, gffn_ref, w1_hbm, w2_hbm, gf_ref, o_ref,
                  mix_scr, hid_scr, k_scr, v_scr, bd_scr, lhs_scr, sv_scr, sc_scr, p_scr, inv_scr, y_scr,
                  wout_ref, w1_ref, w2_ref, w_sem, *, n_tiles, tiles_per_seq):
    i = pl.program_id(0)
    tile_a = jnp.minimum(i, n_tiles - 1)
    tile_b = jnp.maximum(i - 1, 0)

    def mod_rows(tile):
        return [mod_ref[k, pl.ds(tile // tiles_per_seq, 1), :] for k in range(N_MOD)]

    def ffn_stream():
        return _ffn_stream(mod_rows(tile_b), xb_ref, mix_scr, wout_ref, gffn_ref, w1_ref,
                           w2_ref, gf_ref, o_ref, hid_scr, y_scr)

    def mixer_stream():
        s_a = tile_a % tiles_per_seq
        pos_row = pos_ref[pl.ds(tile_a // tiles_per_seq, 1), :]
        return _mixer_stream(s_a, s_a == tiles_per_seq - 1, mod_rows(tile_a), sinks_ref,
                             xa_ref, pos_row, gm_ref, win_ref, ws_ref, sb_ref, invf_ref, mix_scr,
                             k_scr, v_scr, bd_scr, lhs_scr, sv_scr, sc_scr, p_scr, inv_scr)

    def weight_copies():
        pairs = ((wout_hbm, wout_ref), (w1_hbm, w1_ref), (w2_hbm, w2_ref))
        return [pltpu.make_async_copy(src, dst, w_sem.at[k]) for k, (src, dst) in enumerate(pairs)]

    @pl.when(i == 1)
    def _():
        for copy in weight_copies():
            copy.wait()

    @pl.when(i == 0)
    def _():
        for k, copy in enumerate(weight_copies()):
            copy.start(priority=k % 2)
        k_scr[:, 0:Q_BLOCK, :] = jnp.zeros((N_KV_HEADS, Q_BLOCK, LANES), BF16)
        v_scr[:, 0:Q_BLOCK] = jnp.zeros((KV_WIDTH, Q_BLOCK), BF16)
        for _ in mixer_stream():
            pass

    @pl.when(i == n_tiles)
    def _():
        for _ in ffn_stream():
            pass

    @pl.when(jnp.logical_and(i > 0, i < n_tiles))
    def _():
        ffn = ffn_stream()
        mixer = mixer_stream()
        order = ("ab"
                 "ab"
                 "baabaabaa"
                 "ababababab"
                 "ab")
        streams = {"a": mixer, "b": ffn}
        for name in order:
            next(streams[name], None)
        for stream in (mixer, ffn):
            for _ in stream:
                pass


def _inv_freq_column():
    inv_freq = ROPE_THETA ** (-np.arange(0, ROT_DIM, 2, dtype=np.float32) / ROT_DIM)
    return inv_freq.astype(np.float32)[:, None]


def _const_spec(shape):
    zeros = (0,) * len(shape)
    return pl.BlockSpec(shape, lambda *_: zeros, pipeline_mode=pl.Buffered(1))


def kernel(x, c, positions, w_ada, b_ada, g_mix, w_in, w_spatial, b_spatial, sinks, w_out, g_ffn,
           w_ff1, w_ff2, g_final):
    B, S, D = x.shape
    ts = TOKENS_PER_STEP
    assert D == D_MODEL and S % ts == 0 and w_ada.shape[0] == 1

    mod = pl.pallas_call(
        _ada_kernel,
        grid=(D // ADA_ROWS,),
        in_specs=[pl.BlockSpec((B, ADA_ROWS), lambda k: (0, k)),
                  pl.BlockSpec((ADA_ROWS, N_MOD * D), lambda k: (k, 0)),
                  pl.BlockSpec((1, N_MOD * D), lambda k: (0, 0))],
        out_specs=pl.BlockSpec((N_MOD, B, D), lambda k: (0, 0, 0)),
        out_shape=jax.ShapeDtypeStruct((N_MOD, B, D), F32),
        compiler_params=pltpu.CompilerParams(dimension_semantics=("arbitrary",),
                                             vmem_limit_bytes=VMEM_LIMIT_BYTES),
        name="ada_mod",
    )(c, w_ada[0], b_ada)

    sbias = jnp.repeat(b_spatial[0].T, HEAD_DIM, axis=1)

    tiles_per_seq = S // ts
    n_tiles = B * tiles_per_seq

    def tile_a(i):
        return jnp.minimum(i, n_tiles - 1)

    def tile_b(i):
        return jnp.maximum(i - 1, 0)

    def tok_spec(tile):
        return pl.BlockSpec((1, ts, D), lambda i: (tile(i) // tiles_per_seq, tile(i) % tiles_per_seq, 0))

    return pl.pallas_call(
        functools.partial(_block_kernel, n_tiles=n_tiles, tiles_per_seq=tiles_per_seq),
        grid=(n_tiles + 1,),
        in_specs=[pl.BlockSpec(memory_space=pltpu.SMEM),
                  tok_spec(tile_a),
                  tok_spec(tile_b),
                  pl.BlockSpec((B, ts), lambda i: (0, tile_a(i) % tiles_per_seq)),
                  _const_spec((N_MOD, B, D)),
                  _const_spec((1, D)),
                  _const_spec((D, IN_PROJ_WIDTH)),
                  _const_spec((GMLP_GROUPS, CHUNK, CHUNK)),
                  _const_spec((CHUNK, GMLP_WIDTH)),
                  _const_spec((ROT_DIM // 2, 1)),
                  pl.BlockSpec(memory_space=pl.ANY),
                  _const_spec((1, D)),
                  pl.BlockSpec(memory_space=pl.ANY),
                  pl.BlockSpec(memory_space=pl.ANY),
                  _const_spec((1, D))],
        out_specs=tok_spec(tile_b),
        out_shape=jax.ShapeDtypeStruct((B, S, D), F32),
        scratch_shapes=[pltpu.VMEM((ts, D), BF16),
                        pltpu.VMEM((ts, D_FF), BF16),
                        pltpu.VMEM((N_KV_HEADS, Q_BLOCK + ts, LANES), BF16),
                        pltpu.VMEM((KV_WIDTH, Q_BLOCK + ts), BF16),
                        pltpu.VMEM((N_SUB * N_PAIRS, 2 * CHUNK, LANES), BF16),
                        pltpu.VMEM((N_ATT, GQA_GROUP * Q_BLOCK, LANES), BF16),
                        pltpu.VMEM((ts, GMLP_WIDTH), F32),
                        pltpu.VMEM((N_ATT, 2 * Q_BLOCK, GQA_GROUP * Q_BLOCK), F32),
                        pltpu.VMEM((N_ATT, 2 * Q_BLOCK, GQA_GROUP * Q_BLOCK), BF16),
                        pltpu.VMEM((N_ATT, 1, GQA_GROUP * Q_BLOCK), F32),
                        pltpu.VMEM((ts, D), F32),
                        pltpu.VMEM((D, D), BF16),
                        pltpu.VMEM((D, D_FF), BF16),
                        pltpu.VMEM((D_FF, D), BF16),
                        pltpu.SemaphoreType.DMA((3,))],
        compiler_params=pltpu.CompilerParams(dimension_semantics=("arbitrary",),
                                             vmem_limit_bytes=VMEM_LIMIT_BYTES),
        name="decoder_block",
    )(sinks[0], x, x, positions, mod, g_mix, w_in[0].astype(BF16), w_spatial[0], sbias,
      jnp.asarray(_inv_freq_column()), w_out[0].astype(BF16), g_ffn,
      w_ff1[0].astype(BF16), w_ff2[0].astype(BF16), g_final[None])
```
